```python
import math, functools
import jax, jax.numpy as jnp
from jax import lax
import numpy as np

D_MODEL = 1024
BATCH = 32
SEQ = 2048
DEPTH = 1
DEC_BATCH = 8
DEC_SEQ = 2048
PAST_LEN = 128

N_HEADS_A = 16
HEAD_DIM_A = 64
N_KV_A = 4
REP_A = N_HEADS_A // N_KV_A
WINDOW = 128
N_BUCKETS = 32
MAX_DISTANCE = 128
N_HEADS_B = 4
DK_B = D_MODEL // 8
DV_B = D_MODEL // 4
GATE_RANK = 16
GATE_TEMP = 16.0
GLA_CHUNK = 64
N_HEADS_X = 4
HEAD_DIM_X = D_MODEL // N_HEADS_X
N_MEM = 256
N_GROUPS = 4
EXPERTS_PER_GROUP = 8
N_EXPERTS = N_GROUPS * EXPERTS_PER_GROUP
TOP_K = 2
D_EXPERT = D_MODEL // 2
MOE_BLOCK = 256

EPS = 1e-6
NEG_INF = -1e30
IN_SIZES = (N_HEADS_A * HEAD_DIM_A, N_KV_A * HEAD_DIM_A, N_KV_A * HEAD_DIM_A,
            N_HEADS_B * DK_B, N_HEADS_B * DK_B, N_HEADS_B * DV_B, N_HEADS_B * DV_B,
            2 * GATE_RANK, 2 * D_MODEL)
IN_COLS = sum(IN_SIZES)

kernel_name = 'hybrid_swa_gla_hmoe_encoder'


def rmsnorm(x, g):
    xf = x.astype(jnp.float32)
    y = xf * lax.rsqrt(jnp.mean(xf * xf, axis=-1, keepdims=True) + EPS)
    return (y * g.astype(jnp.float32)).astype(x.dtype)


def t5_bucket(rel):
    half = N_BUCKETS // 2
    max_exact = half // 2
    n = jnp.abs(rel)
    large = max_exact + (jnp.log(jnp.maximum(n, 1).astype(jnp.float32) / max_exact)
                         / math.log(MAX_DISTANCE / max_exact) * (half - max_exact)).astype(jnp.int32)
    large = jnp.minimum(large, half - 1)
    return jnp.where(rel > 0, half, 0) + jnp.where(n < max_exact, n, large)


def window_bias(rel_bias):
    r = jnp.arange(WINDOW)[:, None]
    c = jnp.arange(3 * WINDOW)[None, :]
    rel = c - WINDOW - r
    bias = rel_bias.astype(jnp.float32)[t5_bucket(rel)]
    bias = jnp.transpose(bias, (2, 0, 1)).reshape(N_KV_A, REP_A, WINDOW, 3 * WINDOW)
    return bias, jnp.abs(rel) <= WINDOW


def window_attention(q, k, v, bias, in_band, sink):
    B, S = q.shape[0], q.shape[1]
    nb = S // WINDOW
    qb = q.reshape(B, nb, WINDOW, N_KV_A, REP_A, HEAD_DIM_A).swapaxes(0, 1)

    def band(t):
        tp = jnp.pad(t, ((0, 0), (WINDOW, WINDOW), (0, 0), (0, 0)))
        tp = tp.reshape(B, nb + 2, WINDOW, N_KV_A, HEAD_DIM_A)
        return jnp.concatenate([tp[:, :-2], tp[:, 1:-1], tp[:, 2:]], axis=2).swapaxes(0, 1)

    kb, vb = band(k), band(v)
    sink_f = sink.astype(jnp.float32).reshape(N_KV_A, REP_A)[None, :, :, None]
    scale = HEAD_DIM_A ** -0.5

    def attend(args):
        qi, ki, vi, i = args
        s = jnp.einsum('bqgrd,bkgd->bgrqk', qi, ki).astype(jnp.float32) * scale + bias
        kpos = i * WINDOW - WINDOW + jnp.arange(3 * WINDOW)
        valid = in_band & ((kpos >= 0) & (kpos < S))[None, :]
        s = jnp.where(valid, s, NEG_INF)
        m = jnp.maximum(s.max(axis=-1), sink_f)
        p = jnp.exp(s - m[..., None])
        denom = p.sum(axis=-1) + jnp.exp(sink_f - m)
        o = jnp.einsum('bgrqk,bkgd->bgrqd', p.astype(vi.dtype), vi).astype(jnp.float32) / denom[..., None]
        return o.astype(vi.dtype).transpose(0, 3, 1, 2, 4)

    o = lax.map(attend, (qb, kb, vb, jnp.arange(nb)))
    return o.swapaxes(0, 1).reshape(B, S, N_HEADS_A * HEAD_DIM_A)


def gla_scan(q, k, v, g, strict):
    B, S, H = q.shape[0], q.shape[1], q.shape[2]
    nc = S // GLA_CHUNK

    def chunks(t):
        return t.astype(jnp.float32).reshape(B, nc, GLA_CHUNK, H, t.shape[-1]).transpose(1, 0, 3, 2, 4)

    qc, kc, vc, gc = chunks(q), chunks(k), chunks(v), chunks(g)
    b = jnp.cumsum(gc, axis=3)
    b_last = b[:, :, :, -1:, :]
    q_dec = qc * jnp.exp(b)
    k_intra = kc * jnp.exp(-b)
    k_state = kc * jnp.exp(b_last - b)
    idx = jnp.arange(GLA_CHUNK)
    mask = (idx[:, None] > idx[None, :]) if strict else (idx[:, None] >= idx[None, :])
    a = jnp.where(mask, jnp.einsum('nbhid,nbhjd->nbhij', q_dec, k_intra), 0.0)
    o_intra = jnp.einsum('nbhij,nbhjv->nbhiv', a, vc)

    def step(state, inp):
        qd, ks, vv, dl = inp
        o = jnp.einsum('bhid,bhdv->bhiv', qd, state)
        state = state * dl[:, :, 0, :, None] + jnp.einsum('bhjd,bhjv->bhdv', ks, vv)
        return state, o

    state0 = jnp.zeros((B, H, DK_B, DV_B), jnp.float32)
    _, o_inter = lax.scan(step, state0, (q_dec, k_state, vc, jnp.exp(b_last)))
    return (o_intra + o_inter).transpose(1, 0, 3, 2, 4).reshape(B, S, H, DV_B)


def hybrid_mixer(u, bias, in_band, w_in, attn_sink, gla_gate_up, gla_gate_bias, gla_norm,
                 w_branch_a, w_branch_b, w_out):
    B, S, _ = u.shape
    split_at = []
    acc = 0
    for sz in IN_SIZES[:-1]:
        acc += sz
        split_at.append(acc)
    aq, ak, av, bq, bk, bv, br, blr, gates = jnp.split(u @ w_in, split_at, axis=-1)
    ya = window_attention(aq.reshape(B, S, N_HEADS_A, HEAD_DIM_A),
                          ak.reshape(B, S, N_KV_A, HEAD_DIM_A),
                          av.reshape(B, S, N_KV_A, HEAD_DIM_A), bias, in_band, attn_sink)
    ya = ya @ w_branch_a
    q = bq.reshape(B, S, N_HEADS_B, DK_B) * (DK_B ** -0.5)
    k = bk.reshape(B, S, N_HEADS_B, DK_B)
    v = bv.reshape(B, S, N_HEADS_B, DV_B)
    z = jnp.einsum('bsnr,nrk->bsnk', blr.reshape(B, S, 2, GATE_RANK), gla_gate_up) + gla_gate_bias
    g = jax.nn.log_sigmoid(z.astype(jnp.float32)) / GATE_TEMP
    g_f = g[:, :, 0].reshape(B, S, N_HEADS_B, DK_B)
    g_b = g[:, :, 1].reshape(B, S, N_HEADS_B, DK_B)
    fl = lambda t: jnp.flip(t, axis=1)
    o = gla_scan(q, k, v, g_f, False) + fl(gla_scan(fl(q), fl(k), fl(v), fl(g_b), True))
    o = o * lax.rsqrt(jnp.mean(o * o, axis=-1, keepdims=True) + EPS)
    o = o * gla_norm.astype(jnp.float32).reshape(N_HEADS_B, DV_B)
    o = o.astype(u.dtype).reshape(B, S, N_HEADS_B * DV_B) * jax.nn.silu(br)
    yb = o @ w_branch_b
    ga, gb = jnp.split(gates, 2, axis=-1)
    merged = jax.nn.sigmoid(ga) * ya + jax.nn.sigmoid(gb) * yb
    return merged @ w_out


def memory_cross_attention(u, mm, w_xq, w_xkv, w_xo):
    B, S, _ = u.shape
    M = mm.shape[1]
    q = (u @ w_xq).reshape(B, S, N_HEADS_X, HEAD_DIM_X)
    k, v = jnp.split(mm @ w_xkv, 2, axis=-1)
    k = k.reshape(B, M, N_HEADS_X, HEAD_DIM_X)
    v = v.reshape(B, M, N_HEADS_X, HEAD_DIM_X)
    s = jnp.einsum('bqhd,bkhd->bhqk', q, k).astype(jnp.float32) * (HEAD_DIM_X ** -0.5)
    p = jax.nn.softmax(s, axis=-1)
    o = jnp.einsum('bhqk,bkhd->bqhd', p.astype(v.dtype), v).reshape(B, S, D_MODEL)
    return o @ w_xo


def hierarchical_moe(u, w_rg, b_rg, w_re, b_re, w_gate, w_up, w_down):
    B, S, D = u.shape
    T = B * S
    xt = u.reshape(T, D)
    gl = (xt @ w_rg).astype(jnp.float32) + b_rg.astype(jnp.float32)
    p_grp, grp = lax.top_k(jax.nn.softmax(gl, axis=-1), 1)
    el = ((xt @ w_re).astype(jnp.float32) + b_re.astype(jnp.float32)).reshape(T, N_GROUPS, EXPERTS_PER_GROUP)
    el_g = jnp.take_along_axis(el, grp[:, :, None], axis=1)[:, 0]
    top_p, top_i = lax.top_k(jax.nn.softmax(el_g, axis=-1), TOP_K)
    wts = p_grp * top_p / jnp.sum(top_p, axis=-1, keepdims=True)
    eid = grp * EXPERTS_PER_GROUP + top_i
    A = T * TOP_K
    e_flat = eid.reshape(A)
    tok_flat = jnp.repeat(jnp.arange(T, dtype=jnp.int32), TOP_K)
    order = jnp.argsort(e_flat)
    e_s = e_flat[order]
    tok_s = tok_flat[order]
    counts = jnp.zeros((N_EXPERTS,), jnp.int32).at[e_flat].add(1)
    start = jnp.cumsum(counts) - counts
    padded = (counts + MOE_BLOCK - 1) // MOE_BLOCK * MOE_BLOCK
    pend = jnp.cumsum(padded)
    pstart = pend - padded
    dest = pstart[e_s] + (jnp.arange(A, dtype=jnp.int32) - start[e_s])
    n_blocks = (A + MOE_BLOCK - 1) // MOE_BLOCK + N_EXPERTS
    L = n_blocks * MOE_BLOCK
    slot_tok = jnp.full((L,), T, jnp.int32).at[dest].set(tok_s)
    block_e = jnp.minimum(jnp.searchsorted(pend, jnp.arange(n_blocks, dtype=jnp.int32) * MOE_BLOCK,
                                           side='right'), N_EXPERTS - 1)
    x_pad = jnp.concatenate([xt, jnp.zeros((1, D), xt.dtype)], axis=0)

    def expert_block(args):
        toks, e = args
        xb = x_pad[toks]
        h = jax.nn.silu(xb @ w_gate[e]) * (xb @ w_up[e])
        return h @ w_down[e]

    out = lax.map(expert_block, (slot_tok.reshape(n_blocks, MOE_BLOCK), block_e)).reshape(L, D)
    slot_of = jnp.zeros((A,), jnp.int32).at[order].set(dest)
    y = jnp.einsum('tkd,tk->td', out[slot_of].reshape(T, TOP_K, D), wts.astype(out.dtype))
    return y.reshape(B, S, D)


def encoder_trunk(x, mem, rel_bias, norm_mix, w_in, attn_sink, gla_gate_up, gla_gate_bias, gla_norm,
                  w_branch_a, w_branch_b, w_out, norm_xattn, norm_mem, w_xq, w_xkv, w_xo,
                  norm_moe, w_router_group, b_router_group, w_router_expert, b_router_expert,
                  w_expert_gate, w_expert_up, w_expert_down, norm_final):
    bias, in_band = window_bias(rel_bias)
    h = x
    for l in range(DEPTH):
        u = rmsnorm(h, norm_mix[l])
        h = h + hybrid_mixer(u, bias, in_band, w_in[l], attn_sink[l], gla_gate_up[l], gla_gate_bias[l],
                             gla_norm[l], w_branch_a[l], w_branch_b[l], w_out[l])
        u = rmsnorm(h, norm_xattn[l])
        h = h + memory_cross_attention(u, rmsnorm(mem, norm_mem[l]), w_xq[l], w_xkv[l], w_xo[l])
        u = rmsnorm(h, norm_moe[l])
        h = h + hierarchical_moe(u, w_router_group[l], b_router_group[l], w_router_expert[l],
                                 b_router_expert[l], w_expert_gate[l], w_expert_up[l], w_expert_down[l])
    return rmsnorm(h, norm_final)


def setup_inputs(seed: int = 0) -> dict:
    key = jax.random.key(seed)
    ks = iter(jax.random.split(key, 32))

    def nrm(shape, scale):
        return jax.random.normal(next(ks), shape, jnp.float32) * scale

    def gain(shape):
        return 1.0 + 0.02 * jax.random.normal(next(ks), shape, jnp.float32)

    D = D_MODEL
    return {
        'x_prompt': nrm((BATCH, SEQ, D), 1.0),
        'x_sample': nrm((DEC_BATCH, DEC_SEQ, D), 1.0),
        'mem_prompt': nrm((BATCH, N_MEM, D), 1.0),
        'mem_sample': nrm((DEC_BATCH, N_MEM, D), 1.0),
        'rel_bias': nrm((N_BUCKETS, N_HEADS_A), 0.5),
        'norm_mix': gain((DEPTH, D)),
        'w_in': nrm((DEPTH, D, IN_COLS), D ** -0.5),
        'attn_sink': nrm((DEPTH, N_HEADS_A), 1.0),
        'gla_gate_up': nrm((DEPTH, 2, GATE_RANK, N_HEADS_B * DK_B), GATE_RANK ** -0.5),
        'gla_gate_bias': nrm((DEPTH, 2, N_HEADS_B * DK_B), 0.1),
        'gla_norm': gain((DEPTH, N_HEADS_B * DV_B)),
        'w_branch_a': nrm((DEPTH, N_HEADS_A * HEAD_DIM_A, D), (N_HEADS_A * HEAD_DIM_A) ** -0.5),
        'w_branch_b': nrm((DEPTH, N_HEADS_B * DV_B, D), (N_HEADS_B * DV_B) ** -0.5),
        'w_out': nrm((DEPTH, D, D), D ** -0.5),
        'norm_xattn': gain((DEPTH, D)),
        'norm_mem': gain((DEPTH, D)),
        'w_xq': nrm((DEPTH, D, D), D ** -0.5),
        'w_xkv': nrm((DEPTH, D, 2 * D), D ** -0.5),
        'w_xo': nrm((DEPTH, D, D), D ** -0.5),
        'norm_moe': gain((DEPTH, D)),
        'w_router_group': nrm((DEPTH, D, N_GROUPS), D ** -0.5),
        'b_router_group': nrm((DEPTH, N_GROUPS), 0.01),
        'w_router_expert': nrm((DEPTH, D, N_EXPERTS), D ** -0.5),
        'b_router_expert': nrm((DEPTH, N_EXPERTS), 0.01),
        'w_expert_gate': nrm((DEPTH, N_EXPERTS, D, D_EXPERT), D ** -0.5),
        'w_expert_up': nrm((DEPTH, N_EXPERTS, D, D_EXPERT), D ** -0.5),
        'w_expert_down': nrm((DEPTH, N_EXPERTS, D_EXPERT, D), D_EXPERT ** -0.5),
        'norm_final': gain((D,)),
    }


def reference(x_prompt, x_sample, mem_prompt, mem_sample, rel_bias, norm_mix, w_in, attn_sink,
              gla_gate_up, gla_gate_bias, gla_norm, w_branch_a, w_branch_b, w_out, norm_xattn, norm_mem,
              w_xq, w_xkv, w_xo, norm_moe, w_router_group, b_router_group, w_router_expert,
              b_router_expert, w_expert_gate, w_expert_up, w_expert_down, norm_final):
    run = functools.partial(
        encoder_trunk, rel_bias=rel_bias, norm_mix=norm_mix, w_in=w_in, attn_sink=attn_sink,
        gla_gate_up=gla_gate_up, gla_gate_bias=gla_gate_bias, gla_norm=gla_norm,
        w_branch_a=w_branch_a, w_branch_b=w_branch_b, w_out=w_out, norm_xattn=norm_xattn,
        norm_mem=norm_mem, w_xq=w_xq, w_xkv=w_xkv, w_xo=w_xo, norm_moe=norm_moe,
        w_router_group=w_router_group, b_router_group=b_router_group,
        w_router_expert=w_router_expert, b_router_expert=b_router_expert,
        w_expert_gate=w_expert_gate, w_expert_up=w_expert_up, w_expert_down=w_expert_down,
        norm_final=norm_final)
    y_prompt = run(x_prompt, mem_prompt)
    y_sample = run(x_sample, mem_sample)
    return (y_prompt, y_sample)
```

```python
import functools
import math

import jax
import jax.numpy as jnp
from jax import lax
from jax.experimental import pallas as pl
from jax.experimental.pallas import tpu as pltpu

F32 = jnp.float32
BF16 = jnp.bfloat16

D_MODEL = 1024
N_HEADS_A = 16
HEAD_DIM_A = 64
N_KV_A = 4
REP_A = N_HEADS_A // N_KV_A
WINDOW = 128
N_BUCKETS = 32
MAX_DISTANCE = 128
N_HEADS_B = 4
DK_B = 128
DV_B = 256
GATE_RANK = 16
GATE_TEMP = 16.0
GLA_CHUNK = 64
N_HEADS_X = 4
HEAD_DIM_X = D_MODEL // N_HEADS_X
N_GROUPS = 4
EXPERTS_PER_GROUP = 8
N_EXPERTS = N_GROUPS * EXPERTS_PER_GROUP
TOP_K = 2
D_EXPERT = D_MODEL // 2
MOE_ROWS = 256
ROUTER_LANES = 128

EPS = 1e-6
NEG_INF = -1e30

LANE = 128
VMEM_LIMIT = 48 * 1024 * 1024

SEG_WIDTHS = (N_HEADS_A * HEAD_DIM_A, N_KV_A * HEAD_DIM_A, N_KV_A * HEAD_DIM_A,
              N_HEADS_B * DK_B, N_HEADS_B * DK_B, N_HEADS_B * DV_B, N_HEADS_B * DV_B,
              2 * D_MODEL, LANE)
SEG_DTYPES = (BF16, BF16, BF16, BF16, BF16, BF16, BF16, BF16, F32)
BQ_SEG = 3


def _rms(x, g):
    return x * lax.rsqrt(jnp.mean(x * x, axis=-1, keepdims=True) + EPS) * g


def _sigmoid(x):
    return 1.0 / (1.0 + jnp.exp(-x))


def _dot(a, b):
    return jnp.dot(a, b, preferred_element_type=F32)


def _dot_nt(a, b):
    return lax.dot_general(a, b, (((1,), (1,)), ((), ())), preferred_element_type=F32)


def _dot_tn(a, b):
    return lax.dot_general(a, b, (((0,), (0,)), ((), ())), preferred_element_type=F32)


def _params(sem, vmem=VMEM_LIMIT):
    return pltpu.CompilerParams(dimension_semantics=sem, vmem_limit_bytes=vmem)


def _inproj_kernel(x_ref, g_ref, w_ref, *out_refs):
    u = _rms(x_ref[...], g_ref[...]).astype(BF16)
    off = 0
    for seg, (o_ref, width) in enumerate(zip(out_refs, SEG_WIDTHS)):
        for c0 in range(0, width, 1024):
            cw = min(1024, width - c0)
            y = _dot(u, w_ref[:, off + c0:off + c0 + cw])
            if seg == BQ_SEG:
                y = y * (DK_B ** -0.5)
            o_ref[:, c0:c0 + cw] = y.astype(o_ref.dtype)
        off += width


def _inproj(x, g, w, tm=256):
    T = x.shape[0]
    ncols = w.shape[1]
    out_shape = [jax.ShapeDtypeStruct((T, wd), dt) for wd, dt in zip(SEG_WIDTHS, SEG_DTYPES)]
    out_specs = [pl.BlockSpec((tm, wd), lambda i: (i, 0)) for wd in SEG_WIDTHS]
    return pl.pallas_call(
        _inproj_kernel,
        grid=(T // tm,),
        in_specs=[pl.BlockSpec((tm, D_MODEL), lambda i: (i, 0)),
                  pl.BlockSpec((1, D_MODEL), lambda i: (0, 0)),
                  pl.BlockSpec((D_MODEL, ncols), lambda i: (0, 0), pipeline_mode=pl.Buffered(1))],
        out_specs=out_specs,
        out_shape=out_shape,
        compiler_params=_params(("parallel",)),
        name="inproj",
    )(x, g, w)


def _swa_kernel(sink_ref, q_ref, kp_ref, kc_ref, kn_ref, vp_ref, vc_ref, vn_ref, bias_ref, o_ref):
    i = pl.program_id(1)
    nb = pl.num_programs(1)
    r = lax.broadcasted_iota(jnp.int32, (WINDOW, 3 * WINDOW), 0)
    c = lax.broadcasted_iota(jnp.int32, (WINDOW, 3 * WINDOW), 1)
    valid = (c >= r) & (c <= r + 2 * WINDOW)
    valid = valid & ((c >= WINDOW) | (i > 0)) & ((c < 2 * WINDOW) | (i < nb - 1))
    scale = HEAD_DIM_A ** -0.5
    for g in range(N_KV_A):
        ksl = slice(g * HEAD_DIM_A, (g + 1) * HEAD_DIM_A)
        kcat = jnp.concatenate([kp_ref[:, ksl], kc_ref[:, ksl], kn_ref[:, ksl]], axis=0)
        vcat = jnp.concatenate([vp_ref[:, ksl], vc_ref[:, ksl], vn_ref[:, ksl]], axis=0)
        for rep in range(REP_A):
            h = g * REP_A + rep
            hsl = slice(h * HEAD_DIM_A, (h + 1) * HEAD_DIM_A)
            s = _dot_nt(q_ref[:, hsl], kcat) * scale + bias_ref[h]
            s = jnp.where(valid, s, NEG_INF)
            sink = sink_ref[h]
            m = jnp.maximum(jnp.max(s, axis=-1, keepdims=True), sink)
            p = jnp.exp(s - m)
            denom = jnp.sum(p, axis=-1, keepdims=True) + jnp.exp(sink - m)
            o = _dot(p.astype(BF16), vcat) / denom
            o_ref[:, hsl] = o.astype(o_ref.dtype)


def _swa(q, k, v, bias, sink):
    B, S, _ = q.shape
    nb = S // WINDOW
    kvw = N_KV_A * HEAD_DIM_A

    def prev(b, i):
        return (b, jnp.maximum(i - 1, 0), 0)

    def cur(b, i):
        return (b, i, 0)

    def nxt(b, i):
        return (b, jnp.minimum(i + 1, nb - 1), 0)

    kv_specs = [pl.BlockSpec((None, WINDOW, kvw), f) for f in (prev, cur, nxt)]
    return pl.pallas_call(
        _swa_kernel,
        grid=(B, nb),
        in_specs=[pl.BlockSpec(memory_space=pltpu.SMEM),
                  pl.BlockSpec((None, WINDOW, D_MODEL), cur)] + kv_specs + kv_specs +
                 [pl.BlockSpec((N_HEADS_A, WINDOW, 3 * WINDOW), lambda b, i: (0, 0, 0))],
        out_specs=pl.BlockSpec((None, WINDOW, D_MODEL), cur),
        out_shape=jax.ShapeDtypeStruct((B, S, D_MODEL), BF16),
        compiler_params=_params(("parallel", "arbitrary")),
        name="swa",
    )(sink, q, k, k, k, v, v, v, bias)


def _gla_kernel(q_ref, k_ref, v_ref, br_ref, blr_ref, gu_ref, gb_ref, gn_ref, o_ref,
                g_scr, of_scr, st_scr):
    S = q_ref.shape[0]
    C = GLA_CHUNK
    nc = S // C
    lr = blr_ref[...].astype(BF16)
    for d in range(2):
        z = _dot(lr, gu_ref[d]) + gb_ref[d]
        g_scr[d] = (jnp.minimum(z, 0.0) - jnp.log(1.0 + jnp.exp(-jnp.abs(z)))) * (1.0 / GATE_TEMP)

    ri = lax.broadcasted_iota(jnp.int32, (C, C), 0)
    ci = lax.broadcasted_iota(jnp.int32, (C, C), 1)

    def chunk(n, d):
        fwd = d == 0
        sl = pl.ds(pl.multiple_of(n * C, C), C)
        g = g_scr[d, sl, :]
        tri = jnp.where((ri >= ci) if fwd else (ri <= ci), 1.0, 0.0).astype(BF16)
        g0 = g.astype(BF16)
        r1 = g - g0.astype(F32)
        g1 = r1.astype(BF16)
        g2 = (r1 - g1.astype(F32)).astype(BF16)
        b = _dot(tri, g0) + _dot(tri, g1) + _dot(tri, g2)
        b_end = b[C - 1:C, :] if fwd else b[0:1, :]
        q = q_ref[sl, :].astype(F32)
        k = k_ref[sl, :].astype(F32)
        v = v_ref[sl, :]
        qd = (q * jnp.exp(b)).astype(BF16)
        ki = (k * jnp.exp(-b)).astype(BF16)
        ks = (k * jnp.exp(b_end - b)).astype(BF16)
        a = _dot_nt(qd, ki)
        a = jnp.where((ri >= ci) if fwd else (ri < ci), a, 0.0).astype(BF16)
        st = st_scr[...]
        o = _dot(a, v) + _dot_nt(qd, st.astype(BF16))
        st_scr[...] = st * jnp.exp(b_end) + _dot_tn(v, ks)
        return sl, o

    st_scr[...] = jnp.zeros_like(st_scr)

    def fwd_body(n, carry):
        sl, o = chunk(n, 0)
        of_scr[sl, :] = o
        return carry

    lax.fori_loop(0, nc, fwd_body, 0)

    st_scr[...] = jnp.zeros_like(st_scr)

    def bwd_body(t, carry):
        sl, o = chunk(nc - 1 - t, 1)
        o = o + of_scr[sl, :]
        o = o * lax.rsqrt(jnp.mean(o * o, axis=-1, keepdims=True) + EPS) * gn_ref[...]
        r = br_ref[sl, :].astype(F32)
        o_ref[sl, :] = (o * (r * _sigmoid(r))).astype(o_ref.dtype)
        return carry

    lax.fori_loop(0, nc, bwd_body, 0)


def _gla(bq, bk, bv, br, blr, gu, gb, gn):
    B, S, _ = bq.shape
    return pl.pallas_call(
        _gla_kernel,
        grid=(B, N_HEADS_B),
        in_specs=[pl.BlockSpec((None, S, DK_B), lambda b, h: (b, 0, h)),
                  pl.BlockSpec((None, S, DK_B), lambda b, h: (b, 0, h)),
                  pl.BlockSpec((None, S, DV_B), lambda b, h: (b, 0, h)),
                  pl.BlockSpec((None, S, DV_B), lambda b, h: (b, 0, h)),
                  pl.BlockSpec((None, S, LANE), lambda b, h: (b, 0, 0)),
                  pl.BlockSpec((2, None, LANE, DK_B), lambda b, h: (0, h, 0, 0)),
                  pl.BlockSpec((2, None, 1, DK_B), lambda b, h: (0, h, 0, 0)),
                  pl.BlockSpec((None, 1, DV_B), lambda b, h: (h, 0, 0))],
        out_specs=pl.BlockSpec((None, S, DV_B), lambda b, h: (b, 0, h)),
        out_shape=jax.ShapeDtypeStruct((B, S, N_HEADS_B * DV_B), BF16),
        scratch_shapes=[pltpu.VMEM((2, S, DK_B), F32),
                        pltpu.VMEM((S, DV_B), F32),
                        pltpu.VMEM((DV_B, DK_B), F32)],
        compiler_params=_params(("parallel", "arbitrary")),
        name="gla",
    )(bq, bk, bv, br, blr, gu, gb, gn)


def _postmix_kernel(ao_ref, ob_ref, gates_ref, x_ref, wa_ref, wb_ref, wo_ref, h_ref):
    ya = _dot(ao_ref[...], wa_ref[...])
    yb = _dot(ob_ref[...], wb_ref[...])
    ga = gates_ref[:, :D_MODEL].astype(F32)
    gb = gates_ref[:, D_MODEL:].astype(F32)
    merged = _sigmoid(ga) * ya + _sigmoid(gb) * yb
    h_ref[...] = x_ref[...] + _dot(merged.astype(BF16), wo_ref[...])


def _postmix(ao, ob, gates, x, wa, wb, wo, tm=256):
    T = x.shape[0]
    row = lambda i: (i, 0)
    const = lambda i: (0, 0)
    wspec = pl.BlockSpec((D_MODEL, D_MODEL), const)
    return pl.pallas_call(
        _postmix_kernel,
        grid=(T // tm,),
        in_specs=[pl.BlockSpec((tm, D_MODEL), row), pl.BlockSpec((tm, D_MODEL), row),
                  pl.BlockSpec((tm, 2 * D_MODEL), row), pl.BlockSpec((tm, D_MODEL), row),
                  wspec, wspec, wspec],
        out_specs=pl.BlockSpec((tm, D_MODEL), row),
        out_shape=jax.ShapeDtypeStruct((T, D_MODEL), F32),
        compiler_params=_params(("parallel",)),
        name="postmix",
    )(ao, ob, gates, x, wa, wb, wo)


def _memkv_kernel(m_ref, g_ref, w_ref, k_ref, v_ref):
    u = _rms(m_ref[...], g_ref[...]).astype(BF16)
    k_ref[...] = _dot(u, w_ref[:, :D_MODEL]).astype(k_ref.dtype)
    v_ref[...] = _dot(u, w_ref[:, D_MODEL:]).astype(v_ref.dtype)


def _memkv(mem, g, w):
    B, M, _ = mem.shape
    blk = pl.BlockSpec((None, M, D_MODEL), lambda b: (b, 0, 0))
    return pl.pallas_call(
        _memkv_kernel,
        grid=(B,),
        in_specs=[blk, pl.BlockSpec((1, D_MODEL), lambda b: (0, 0)),
                  pl.BlockSpec((D_MODEL, 2 * D_MODEL), lambda b: (0, 0))],
        out_specs=[blk, blk],
        out_shape=[jax.ShapeDtypeStruct((B, M, D_MODEL), BF16)] * 2,
        compiler_params=_params(("parallel",)),
        name="memkv",
    )(mem, g, w)


def _xattn_kernel(h_ref, k_ref, v_ref, gx_ref, wq_ref, wo_ref, gm_ref, wr_hi_ref, wr_lo_ref, br_ref,
                  h2_ref, u3_ref, lg_ref):
    h1 = h_ref[...]
    u2 = _rms(h1, gx_ref[...]).astype(BF16)
    qx = _dot(u2, wq_ref[...]).astype(BF16)
    scale = HEAD_DIM_X ** -0.5
    outs = []
    for hh in range(N_HEADS_X):
        sl = slice(hh * HEAD_DIM_X, (hh + 1) * HEAD_DIM_X)
        s = _dot_nt(qx[:, sl], k_ref[:, sl]) * scale
        m = jnp.max(s, axis=-1, keepdims=True)
        p = jnp.exp(s - m)
        p = p / jnp.sum(p, axis=-1, keepdims=True)
        outs.append(_dot(p.astype(BF16), v_ref[:, sl]).astype(BF16))
    o = jnp.concatenate(outs, axis=-1)
    h2 = h1 + _dot(o, wo_ref[...])
    h2_ref[...] = h2
    u3 = _rms(h2, gm_ref[...])
    u3_ref[...] = u3
    u_hi = u3.astype(BF16)
    u_lo = (u3 - u_hi.astype(F32)).astype(BF16)
    lg = _dot(u_hi, wr_hi_ref[...]) + (_dot(u_hi, wr_lo_ref[...]) + _dot(u_lo, wr_hi_ref[...]))
    lg_ref[...] = lg + br_ref[...]


def _xattn(h1, mk, mv, gx, wq, wo, gm, wr_hi, wr_lo, brt, tm=256):
    B, S, _ = h1.shape
    M = mk.shape[1]
    row = lambda b, i: (b, i, 0)
    c2 = lambda b, i: (0, 0)
    kv = pl.BlockSpec((None, M, D_MODEL), lambda b, i: (b, 0, 0))
    vec = pl.BlockSpec((1, D_MODEL), c2)
    wsq = pl.BlockSpec((D_MODEL, D_MODEL), c2)
    wr = pl.BlockSpec((D_MODEL, ROUTER_LANES), c2)
    return pl.pallas_call(
        _xattn_kernel,
        grid=(B, S // tm),
        in_specs=[pl.BlockSpec((None, tm, D_MODEL), row), kv, kv, vec, wsq, wsq, vec, wr, wr,
                  pl.BlockSpec((1, ROUTER_LANES), c2)],
        out_specs=[pl.BlockSpec((None, tm, D_MODEL), row), pl.BlockSpec((None, tm, D_MODEL), row),
                   pl.BlockSpec((None, tm, ROUTER_LANES), row)],
        out_shape=[jax.ShapeDtypeStruct((B, S, D_MODEL), F32), jax.ShapeDtypeStruct((B, S, D_MODEL), F32),
                   jax.ShapeDtypeStruct((B, S, ROUTER_LANES), F32)],
        compiler_params=_params(("parallel", "arbitrary")),
        name="xattn",
    )(h1, mk, mv, gx, wq, wo, gm, wr_hi, wr_lo, brt)


def _moe_kernel(be_ref, nrows_ref, src_ref, dst_ref, u_hbm, wg_ref, wu_ref, wd_ref, o_hbm,
                xbuf, obuf, sem_in, sem_out):
    j = pl.program_id(0)
    nrows = nrows_ref[j]

    def gather_copy(r):
        t = src_ref[0, 0, r]
        return pltpu.make_async_copy(u_hbm.at[pl.ds(t, 1), :], xbuf.at[pl.ds(r, 1), :], sem_in)

    def scatter_copy(r):
        t = dst_ref[0, 0, r]
        return pltpu.make_async_copy(obuf.at[pl.ds(r, 1), :], o_hbm.at[pl.ds(t, 1), :], sem_out)

    def each_row(n, fn):
        def body(r, carry):
            fn(r)
            return carry
        lax.fori_loop(0, n, body, 0)

    @pl.when(nrows > 0)
    def _():
        each_row(MOE_ROWS, lambda r: gather_copy(r).start())
        each_row(MOE_ROWS, lambda r: gather_copy(r).wait())
        x = xbuf[...].astype(BF16)
        g = _dot(x, wg_ref[...])
        u = _dot(x, wu_ref[...])
        hid = (g * _sigmoid(g)) * u
        obuf[...] = _dot(hid.astype(BF16), wd_ref[...])
        each_row(nrows, lambda r: scatter_copy(r).start())
        each_row(nrows, lambda r: scatter_copy(r).wait())


def _moe(block_e, n_rows, src, dst, u3, wg, wu, wd, n_out_rows):
    n_blocks = src.shape[0]
    idx = pl.BlockSpec((1, 1, MOE_ROWS), lambda j, be, nu: (j, 0, 0), memory_space=pltpu.SMEM)
    grid_spec = pltpu.PrefetchScalarGridSpec(
        num_scalar_prefetch=2,
        grid=(n_blocks,),
        in_specs=[idx, idx,
                  pl.BlockSpec(memory_space=pl.ANY),
                  pl.BlockSpec((None, D_MODEL, D_EXPERT), lambda j, be, nu: (be[j], 0, 0)),
                  pl.BlockSpec((None, D_MODEL, D_EXPERT), lambda j, be, nu: (be[j], 0, 0)),
                  pl.BlockSpec((None, D_EXPERT, D_MODEL), lambda j, be, nu: (be[j], 0, 0))],
        out_specs=pl.BlockSpec(memory_space=pl.ANY),
        scratch_shapes=[pltpu.VMEM((MOE_ROWS, D_MODEL), F32), pltpu.VMEM((MOE_ROWS, D_MODEL), F32),
                        pltpu.SemaphoreType.DMA, pltpu.SemaphoreType.DMA])
    return pl.pallas_call(
        _moe_kernel,
        grid_spec=grid_spec,
        out_shape=jax.ShapeDtypeStruct((n_out_rows, D_MODEL), F32),
        compiler_params=_params(("arbitrary",)),
        name="moe",
    )(block_e, n_rows, src, dst, u3, wg, wu, wd)


def _combine_kernel(o_ref, w_ref, h_ref, g_ref, y_ref):
    w = w_ref[...]
    y = o_ref[:, :D_MODEL] * w[:, 0:1] + o_ref[:, D_MODEL:] * w[:, 1:2]
    y_ref[...] = _rms(h_ref[...] + y, g_ref[...])


def _combine(o2, wts, h2, g, tm=512):
    T = h2.shape[0]
    row = lambda i: (i, 0)
    return pl.pallas_call(
        _combine_kernel,
        grid=(T // tm,),
        in_specs=[pl.BlockSpec((tm, 2 * D_MODEL), row), pl.BlockSpec((tm, TOP_K), row),
                  pl.BlockSpec((tm, D_MODEL), row), pl.BlockSpec((1, D_MODEL), lambda i: (0, 0))],
        out_specs=pl.BlockSpec((tm, D_MODEL), row),
        out_shape=jax.ShapeDtypeStruct((T, D_MODEL), F32),
        compiler_params=_params(("parallel",)),
        name="combine",
    )(o2, wts, h2, g)


def _t5_bucket(rel):
    half = N_BUCKETS // 2
    max_exact = half // 2
    n = jnp.abs(rel)
    large = max_exact + (jnp.log(jnp.maximum(n, 1).astype(jnp.float32) / max_exact)
                         / math.log(MAX_DISTANCE / max_exact) * (half - max_exact)).astype(jnp.int32)
    large = jnp.minimum(large, half - 1)
    return jnp.where(rel > 0, half, 0) + jnp.where(n < max_exact, n, large)


def _window_bias(rel_bias):
    r = jnp.arange(WINDOW)[:, None]
    c = jnp.arange(3 * WINDOW)[None, :]
    bias = rel_bias.astype(F32)[_t5_bucket(c - WINDOW - r)]
    return jnp.transpose(bias, (2, 0, 1))


def _split_cols(w, sizes):
    out, acc = [], 0
    for s in sizes:
        out.append(w[:, acc:acc + s])
        acc += s
    return out


def _route(logits, T):
    gl = logits[:, :N_GROUPS]
    el = logits[:, N_GROUPS:N_GROUPS + N_EXPERTS].reshape(T, N_GROUPS, EXPERTS_PER_GROUP)
    p_grp, grp = lax.top_k(jax.nn.softmax(gl, axis=-1), 1)
    el_g = jnp.take_along_axis(el, grp[:, :, None], axis=1)[:, 0]
    top_p, top_i = lax.top_k(jax.nn.softmax(el_g, axis=-1), TOP_K)
    wts = p_grp * top_p / jnp.sum(top_p, axis=-1, keepdims=True)
    eid = grp * EXPERTS_PER_GROUP + top_i

    A = T * TOP_K
    e_flat = eid.reshape(A).astype(jnp.int32)
    order = jnp.argsort(e_flat, stable=True).astype(jnp.int32)
    counts = jnp.sum((e_flat[:, None] == jnp.arange(N_EXPERTS, dtype=jnp.int32)[None, :]).astype(jnp.int32), axis=0)
    start = jnp.cumsum(counts) - counts
    padded = (counts + MOE_ROWS - 1) // MOE_ROWS * MOE_ROWS
    pend = jnp.cumsum(padded)
    pstart = pend - padded
    n_blocks = A // MOE_ROWS + N_EXPERTS
    blk_first = jnp.arange(n_blocks, dtype=jnp.int32) * MOE_ROWS
    block_e = jnp.minimum(jnp.searchsorted(pend, blk_first, side='right'), N_EXPERTS - 1).astype(jnp.int32)
    p0 = blk_first - pstart[block_e]
    n_rows = jnp.clip(counts[block_e] - p0, 0, MOE_ROWS).astype(jnp.int32)
    base = jnp.clip(start[block_e] + p0, 0, A - 1)
    order_pad = jnp.concatenate([order, jnp.zeros((MOE_ROWS,), jnp.int32)])
    win = jax.vmap(lambda s: lax.dynamic_slice(order_pad, (s,), (MOE_ROWS,)))(base)
    valid = jnp.arange(MOE_ROWS, dtype=jnp.int32)[None, :] < n_rows[:, None]
    dst = jnp.where(valid, win, 0).astype(jnp.int32)
    src = dst // TOP_K
    return (wts, block_e, n_rows, src.reshape(n_blocks, 1, MOE_ROWS), dst.reshape(n_blocks, 1, MOE_ROWS))


def kernel(x_prompt, x_sample, mem_prompt, mem_sample, rel_bias, norm_mix, w_in, attn_sink, gla_gate_up, gla_gate_bias, gla_norm, w_branch_a, w_branch_b, w_out, norm_xattn, norm_mem, w_xq, w_xkv, w_xo, norm_moe, w_router_group, b_router_group, w_router_expert, b_router_expert, w_expert_gate, w_expert_up, w_expert_down, norm_final):
    assert norm_mix.shape[0] == 1, "single-layer trunk"
    Bp, S, D = x_prompt.shape
    Bs = x_sample.shape[0]
    assert x_sample.shape[1] == S and D == D_MODEL
    B = Bp + Bs
    T = B * S
    l = 0

    x = jnp.concatenate([x_prompt, x_sample], axis=0).reshape(T, D)
    mem = jnp.concatenate([mem_prompt, mem_sample], axis=0)

    in_sizes = (N_HEADS_A * HEAD_DIM_A, N_KV_A * HEAD_DIM_A, N_KV_A * HEAD_DIM_A,
                N_HEADS_B * DK_B, N_HEADS_B * DK_B, N_HEADS_B * DV_B, N_HEADS_B * DV_B,
                2 * GATE_RANK, 2 * D_MODEL)
    w_aq, w_ak, w_av, w_bq, w_bk, w_bv, w_br, w_blr, w_g = _split_cols(w_in[l], in_sizes)
    w_blr = jnp.pad(w_blr, ((0, 0), (0, LANE - 2 * GATE_RANK)))
    w1 = jnp.concatenate([w_aq, w_ak, w_av, w_bq, w_bk, w_bv, w_br, w_g, w_blr], axis=1).astype(BF16)
    gu = gla_gate_up[l].reshape(2, GATE_RANK, N_HEADS_B, DK_B).transpose(0, 2, 1, 3)
    gu_pad = jnp.zeros((2, N_HEADS_B, LANE, DK_B), F32)
    gu_pad = gu_pad.at[0, :, 0:GATE_RANK].set(gu[0]).at[1, :, GATE_RANK:2 * GATE_RANK].set(gu[1]).astype(BF16)
    gb = gla_gate_bias[l].reshape(2, N_HEADS_B, 1, DK_B).astype(F32)
    gn = gla_norm[l].reshape(N_HEADS_B, 1, DV_B).astype(F32)
    bias = _window_bias(rel_bias)
    w_r = jnp.concatenate([w_router_group[l], w_router_expert[l]], axis=1).astype(F32)
    w_r = jnp.pad(w_r, ((0, 0), (0, ROUTER_LANES - w_r.shape[1])))
    w_r_hi = w_r.astype(BF16)
    w_r_lo = (w_r - w_r_hi.astype(F32)).astype(BF16)
    b_r = jnp.concatenate([b_router_group[l], b_router_expert[l]]).astype(F32)
    b_r = jnp.pad(b_r, (0, ROUTER_LANES - b_r.shape[0])).reshape(1, ROUTER_LANES)

    aq, ak, av, bq, bk, bv, br, gates, blr = _inproj(x, norm_mix[l].reshape(1, D), w1)
    r3 = lambda t: t.reshape(B, S, t.shape[-1])
    ao = _swa(r3(aq), r3(ak), r3(av), bias, attn_sink[l].astype(F32))
    ob = _gla(r3(bq), r3(bk), r3(bv), r3(br), r3(blr), gu_pad, gb, gn)
    h1 = _postmix(ao.reshape(T, D), ob.reshape(T, D), gates, x,
                  w_branch_a[l].astype(BF16), w_branch_b[l].astype(BF16), w_out[l].astype(BF16))

    mk, mv = _memkv(mem, norm_mem[l].reshape(1, D), w_xkv[l].astype(BF16))
    h2, u3, logits = _xattn(h1.reshape(B, S, D), mk, mv, norm_xattn[l].reshape(1, D),
                            w_xq[l].astype(BF16), w_xo[l].astype(BF16), norm_moe[l].reshape(1, D),
                            w_r_hi, w_r_lo, b_r)
    h2 = h2.reshape(T, D)
    u3 = u3.reshape(T, D)

    wts, block_e, n_rows, src, dst = _route(logits.reshape(T, ROUTER_LANES), T)
    n_out_rows = T * TOP_K
    o2 = _moe(block_e, n_rows, src, dst, u3,
              w_expert_gate[l].astype(BF16), w_expert_up[l].astype(BF16), w_expert_down[l].astype(BF16),
              n_out_rows)
    y = _combine(o2.reshape(n_out_rows // TOP_K, TOP_K * D), wts.astype(F32), h2, norm_final.reshape(1, D))
    y = y.reshape(B, S, D)
    return (y[:Bp], y[Bp:])
```

```python
import functools
import math

import jax
import jax.numpy as jnp
from jax import lax
from jax.experimental import pallas as pl
from jax.experimental.pallas import tpu as pltpu

F32 = jnp.float32
BF16 = jnp.bfloat16

D_MODEL = 1024
N_HEADS_A = 16
HEAD_DIM_A = 64
N_KV_A = 4
REP_A = N_HEADS_A // N_KV_A
WINDOW = 128
N_BUCKETS = 32
MAX_DISTANCE = 128
N_HEADS_B = 4
DK_B = 128
DV_B = 256
GATE_RANK = 16
GATE_TEMP = 16.0
GLA_CHUNK = 64
N_HEADS_X = 4
HEAD_DIM_X = D_MODEL // N_HEADS_X
N_GROUPS = 4
EXPERTS_PER_GROUP = 8
N_EXPERTS = N_GROUPS * EXPERTS_PER_GROUP
TOP_K = 2
D_EXPERT = D_MODEL // 2
MOE_ROWS = 256
ROUTER_LANES = 128

EPS = 1e-6
NEG_INF = -1e30

LANE = 128
VMEM_LIMIT = 48 * 1024 * 1024

SEG_WIDTHS = (N_HEADS_A * HEAD_DIM_A, N_KV_A * HEAD_DIM_A, N_KV_A * HEAD_DIM_A,
              N_HEADS_B * DK_B, N_HEADS_B * DK_B, N_HEADS_B * DV_B, N_HEADS_B * DV_B,
              2 * D_MODEL, LANE)
SEG_DTYPES = (BF16, BF16, BF16, BF16, BF16, BF16, BF16, BF16, F32)
BQ_SEG = 3


def _rms(x, g):
    return x * lax.rsqrt(jnp.mean(x * x, axis=-1, keepdims=True) + EPS) * g


def _sigmoid(x):
    return 1.0 / (1.0 + jnp.exp(-x))


def _dot(a, b):
    return jnp.dot(a, b, preferred_element_type=F32)


def _dot_nt(a, b):
    return lax.dot_general(a, b, (((1,), (1,)), ((), ())), preferred_element_type=F32)


def _dot_tn(a, b):
    return lax.dot_general(a, b, (((0,), (0,)), ((), ())), preferred_element_type=F32)


def _params(sem, vmem=VMEM_LIMIT):
    return pltpu.CompilerParams(dimension_semantics=sem, vmem_limit_bytes=vmem)


def _two_part_specs(n_first, n_second, tm):
    return [pl.BlockSpec((tm, D_MODEL), lambda i: (jnp.minimum(i, n_first - 1), 0)),
            pl.BlockSpec((tm, D_MODEL), lambda i: (jnp.clip(i - n_first, 0, n_second - 1), 0))]


def _two_part_rows(n_first, a_ref, b_ref):
    return jnp.where(pl.program_id(0) < n_first, a_ref[...], b_ref[...])


def _inproj_kernel(n_first, xa_ref, xb_ref, g_ref, w_ref, *out_refs):
    u = _rms(_two_part_rows(n_first, xa_ref, xb_ref), g_ref[...]).astype(BF16)
    off = 0
    for seg, (o_ref, width) in enumerate(zip(out_refs, SEG_WIDTHS)):
        for c0 in range(0, width, 1024):
            cw = min(1024, width - c0)
            y = _dot(u, w_ref[:, off + c0:off + c0 + cw])
            if seg == BQ_SEG:
                y = y * (DK_B ** -0.5)
            o_ref[:, c0:c0 + cw] = y.astype(o_ref.dtype)
        off += width


def _inproj(xa, xb, g, w, tm=256):
    na, nb = xa.shape[0] // tm, xb.shape[0] // tm
    T = (na + nb) * tm
    ncols = w.shape[1]
    out_shape = [jax.ShapeDtypeStruct((T, wd), dt) for wd, dt in zip(SEG_WIDTHS, SEG_DTYPES)]
    out_specs = [pl.BlockSpec((tm, wd), lambda i: (i, 0)) for wd in SEG_WIDTHS]
    return pl.pallas_call(
        functools.partial(_inproj_kernel, na),
        grid=(na + nb,),
        in_specs=_two_part_specs(na, nb, tm) +
                 [pl.BlockSpec((1, D_MODEL), lambda i: (0, 0)),
                  pl.BlockSpec((D_MODEL, ncols), lambda i: (0, 0), pipeline_mode=pl.Buffered(1))],
        out_specs=out_specs,
        out_shape=out_shape,
        compiler_params=_params(("parallel",)),
        name="inproj",
    )(xa, xb, g, w)


def _swa_kernel(sink_ref, q_ref, kp_ref, kc_ref, kn_ref, vp_ref, vc_ref, vn_ref, bias_ref, o_ref):
    i = pl.program_id(1)
    nb = pl.num_programs(1)
    r = lax.broadcasted_iota(jnp.int32, (WINDOW, 3 * WINDOW), 0)
    c = lax.broadcasted_iota(jnp.int32, (WINDOW, 3 * WINDOW), 1)
    valid = (c >= r) & (c <= r + 2 * WINDOW)
    valid = valid & ((c >= WINDOW) | (i > 0)) & ((c < 2 * WINDOW) | (i < nb - 1))
    scale = HEAD_DIM_A ** -0.5
    for g in range(N_KV_A):
        ksl = slice(g * HEAD_DIM_A, (g + 1) * HEAD_DIM_A)
        kcat = jnp.concatenate([kp_ref[:, ksl], kc_ref[:, ksl], kn_ref[:, ksl]], axis=0)
        vcat = jnp.concatenate([vp_ref[:, ksl], vc_ref[:, ksl], vn_ref[:, ksl]], axis=0)
        for rep in range(REP_A):
            h = g * REP_A + rep
            hsl = slice(h * HEAD_DIM_A, (h + 1) * HEAD_DIM_A)
            s = _dot_nt(q_ref[:, hsl], kcat) * scale + bias_ref[h]
            s = jnp.where(valid, s, NEG_INF)
            sink = sink_ref[h]
            m = jnp.maximum(jnp.max(s, axis=-1, keepdims=True), sink)
            p = jnp.exp(s - m)
            denom = jnp.sum(p, axis=-1, keepdims=True) + jnp.exp(sink - m)
            o = _dot(p.astype(BF16), vcat) / denom
            o_ref[:, hsl] = o.astype(o_ref.dtype)


def _swa(q, k, v, bias, sink):
    B, S, _ = q.shape
    nb = S // WINDOW
    kvw = N_KV_A * HEAD_DIM_A

    def prev(b, i):
        return (b, jnp.maximum(i - 1, 0), 0)

    def cur(b, i):
        return (b, i, 0)

    def nxt(b, i):
        return (b, jnp.minimum(i + 1, nb - 1), 0)

    kv_specs = [pl.BlockSpec((None, WINDOW, kvw), f) for f in (prev, cur, nxt)]
    return pl.pallas_call(
        _swa_kernel,
        grid=(B, nb),
        in_specs=[pl.BlockSpec(memory_space=pltpu.SMEM),
                  pl.BlockSpec((None, WINDOW, D_MODEL), cur)] + kv_specs + kv_specs +
                 [pl.BlockSpec((N_HEADS_A, WINDOW, 3 * WINDOW), lambda b, i: (0, 0, 0))],
        out_specs=pl.BlockSpec((None, WINDOW, D_MODEL), cur),
        out_shape=jax.ShapeDtypeStruct((B, S, D_MODEL), BF16),
        compiler_params=_params(("parallel", "arbitrary")),
        name="swa",
    )(sink, q, k, k, k, v, v, v, bias)


GLA_ROWS = 512
GLA_UNROLL = 8


def _chunk_scan(g, pos, fwd):
    n = g.shape[0]
    s = 1
    while s < GLA_CHUNK:
        if fwd:
            g = g + jnp.where(pos >= s, pltpu.roll(g, s, axis=0), 0.0)
        else:
            g = g + jnp.where(pos < GLA_CHUNK - s, pltpu.roll(g, n - s, axis=0), 0.0)
        s *= 2
    return g


def _gla_kernel(q_ref, k_ref, v_ref, br_ref, blr_ref, gu_ref, gb_ref, gn_ref, o_ref,
                qd_scr, a_scr, kv_scr, dl_scr, acc_scr, st_scr):
    S = q_ref.shape[0]
    C = GLA_CHUNK
    nc = S // C
    R = GLA_ROWS
    cpr = R // C
    ri = lax.broadcasted_iota(jnp.int32, (C, C), 0)
    ci = lax.broadcasted_iota(jnp.int32, (C, C), 1)
    pos = lax.broadcasted_iota(jnp.int32, (R, DK_B), 0) & (C - 1)

    for d in range(2):
        fwd = d == 0

        def prep(t, carry):
            sl = pl.ds(pl.multiple_of(t * R, R), R)
            z = _dot(blr_ref[sl, :].astype(BF16), gu_ref[d]) + gb_ref[d]
            g = (jnp.minimum(z, 0.0) - jnp.log(1.0 + jnp.exp(-jnp.abs(z)))) * (1.0 / GATE_TEMP)
            b = _chunk_scan(g, pos, fwd).reshape(cpr, C, DK_B)
            b_end = b[:, C - 1:C, :] if fwd else b[:, 0:1, :]
            q = q_ref[sl, :].astype(F32).reshape(cpr, C, DK_B)
            k = k_ref[sl, :].astype(F32).reshape(cpr, C, DK_B)
            qd = (q * jnp.exp(b)).astype(BF16)
            ki = (k * jnp.exp(-b)).astype(BF16)
            ks = (k * jnp.exp(b_end - b)).astype(BF16)
            qd_scr[sl, :] = qd.reshape(R, DK_B)
            dl_scr[pl.ds(pl.multiple_of(t * cpr, cpr), cpr)] = jnp.broadcast_to(jnp.exp(b_end), (cpr, 8, DK_B))
            for c in range(cpr):
                csl = pl.ds(pl.multiple_of(t * R + c * C, C), C)
                a = _dot_nt(qd[c], ki[c])
                a_scr[csl, :] = jnp.where((ri >= ci) if fwd else (ri < ci), a, 0.0).astype(BF16)
                kv_scr[t * cpr + c] = _dot_tn(v_ref[csl, :], ks[c])
            return carry

        lax.fori_loop(0, S // R, prep, 0)
        st_scr[...] = jnp.zeros_like(st_scr)

        def chunk(t, carry):
            n = t if fwd else nc - 1 - t
            sl = pl.ds(pl.multiple_of(n * C, C), C)
            st = st_scr[...]
            o = _dot(a_scr[sl, :], v_ref[sl, :]) + _dot_nt(qd_scr[sl, :], st.astype(BF16))
            st_scr[...] = st * dl_scr[n][0:1, :] + kv_scr[n]
            if fwd:
                acc_scr[sl, :] = o
            else:
                o = o + acc_scr[sl, :]
                o = o * lax.rsqrt(jnp.mean(o * o, axis=-1, keepdims=True) + EPS) * gn_ref[...]
                r = br_ref[sl, :].astype(F32)
                o_ref[sl, :] = (o * (r * _sigmoid(r))).astype(o_ref.dtype)
            return carry

        lax.fori_loop(0, nc, chunk, 0, unroll=GLA_UNROLL)


def _gla(bq, bk, bv, br, blr, gu, gb, gn):
    B, S, _ = bq.shape
    nc = S // GLA_CHUNK
    return pl.pallas_call(
        _gla_kernel,
        grid=(B, N_HEADS_B),
        in_specs=[pl.BlockSpec((None, S, DK_B), lambda b, h: (b, 0, h)),
                  pl.BlockSpec((None, S, DK_B), lambda b, h: (b, 0, h)),
                  pl.BlockSpec((None, S, DV_B), lambda b, h: (b, 0, h)),
                  pl.BlockSpec((None, S, DV_B), lambda b, h: (b, 0, h)),
                  pl.BlockSpec((None, S, LANE), lambda b, h: (b, 0, 0)),
                  pl.BlockSpec((2, None, LANE, DK_B), lambda b, h: (0, h, 0, 0)),
                  pl.BlockSpec((2, None, 1, DK_B), lambda b, h: (0, h, 0, 0)),
                  pl.BlockSpec((None, 1, DV_B), lambda b, h: (h, 0, 0))],
        out_specs=pl.BlockSpec((None, S, DV_B), lambda b, h: (b, 0, h)),
        out_shape=jax.ShapeDtypeStruct((B, S, N_HEADS_B * DV_B), BF16),
        scratch_shapes=[pltpu.VMEM((S, DK_B), BF16), pltpu.VMEM((S, GLA_CHUNK), BF16),
                        pltpu.VMEM((nc, DV_B, DK_B), F32),
                        pltpu.VMEM((nc, 8, DK_B), F32),
                        pltpu.VMEM((S, DV_B), F32),
                        pltpu.VMEM((DV_B, DK_B), F32)],
        compiler_params=_params(("parallel", "arbitrary")),
        name="gla",
    )(bq, bk, bv, br, blr, gu, gb, gn)


def _postmix_kernel(n_first, ao_ref, ob_ref, gates_ref, xa_ref, xb_ref, wa_ref, wb_ref, wo_ref, h_ref):
    ya = _dot(ao_ref[...], wa_ref[...])
    yb = _dot(ob_ref[...], wb_ref[...])
    ga = gates_ref[:, :D_MODEL].astype(F32)
    gb = gates_ref[:, D_MODEL:].astype(F32)
    merged = _sigmoid(ga) * ya + _sigmoid(gb) * yb
    h_ref[...] = _two_part_rows(n_first, xa_ref, xb_ref) + _dot(merged.astype(BF16), wo_ref[...])


def _postmix(ao, ob, gates, xa, xb, wa, wb, wo, tm=256):
    na, nb = xa.shape[0] // tm, xb.shape[0] // tm
    T = (na + nb) * tm
    row = lambda i: (i, 0)
    const = lambda i: (0, 0)
    wspec = pl.BlockSpec((D_MODEL, D_MODEL), const)
    return pl.pallas_call(
        functools.partial(_postmix_kernel, na),
        grid=(na + nb,),
        in_specs=[pl.BlockSpec((tm, D_MODEL), row), pl.BlockSpec((tm, D_MODEL), row),
                  pl.BlockSpec((tm, 2 * D_MODEL), row)] + _two_part_specs(na, nb, tm) +
                 [wspec, wspec, wspec],
        out_specs=pl.BlockSpec((tm, D_MODEL), row),
        out_shape=jax.ShapeDtypeStruct((T, D_MODEL), F32),
        compiler_params=_params(("parallel",)),
        name="postmix",
    )(ao, ob, gates, xa, xb, wa, wb, wo)


def _memkv_kernel(m_ref, g_ref, w_ref, k_ref, v_ref):
    u = _rms(m_ref[...], g_ref[...]).astype(BF16)
    k_ref[...] = _dot(u, w_ref[:, :D_MODEL]).astype(k_ref.dtype)
    v_ref[...] = _dot(u, w_ref[:, D_MODEL:]).astype(v_ref.dtype)


def _memkv(mem, g, w):
    B, M, _ = mem.shape
    blk = pl.BlockSpec((None, M, D_MODEL), lambda b: (b, 0, 0))
    return pl.pallas_call(
        _memkv_kernel,
        grid=(B,),
        in_specs=[blk, pl.BlockSpec((1, D_MODEL), lambda b: (0, 0)),
                  pl.BlockSpec((D_MODEL, 2 * D_MODEL), lambda b: (0, 0))],
        out_specs=[blk, blk],
        out_shape=[jax.ShapeDtypeStruct((B, M, D_MODEL), BF16)] * 2,
        compiler_params=_params(("parallel",)),
        name="memkv",
    )(mem, g, w)


def _xattn_kernel(h_ref, k_ref, v_ref, gx_ref, wq_ref, wo_ref, gm_ref, wr_hi_ref, wr_lo_ref, br_ref,
                  h2_ref, u3_ref, rt_ref):
    h1 = h_ref[...]
    u2 = _rms(h1, gx_ref[...]).astype(BF16)
    qx = _dot(u2, wq_ref[...]).astype(BF16)
    scale = HEAD_DIM_X ** -0.5
    outs = []
    for hh in range(N_HEADS_X):
        sl = slice(hh * HEAD_DIM_X, (hh + 1) * HEAD_DIM_X)
        s = _dot_nt(qx[:, sl], k_ref[:, sl]) * scale
        m = jnp.max(s, axis=-1, keepdims=True)
        p = jnp.exp(s - m)
        p = p / jnp.sum(p, axis=-1, keepdims=True)
        outs.append(_dot(p.astype(BF16), v_ref[:, sl]).astype(BF16))
    o = jnp.concatenate(outs, axis=-1)
    h2 = h1 + _dot(o, wo_ref[...])
    h2_ref[...] = h2
    u3 = _rms(h2, gm_ref[...])
    u3_ref[...] = u3
    u_hi = u3.astype(BF16)
    u_lo = (u3 - u_hi.astype(F32)).astype(BF16)
    lg = _dot(u_hi, wr_hi_ref[...]) + (_dot(u_hi, wr_lo_ref[...]) + _dot(u_lo, wr_hi_ref[...]))
    rt_ref[...] = _route_lanes(lg + br_ref[...])


def _route_lanes(lg):
    lane = lax.broadcasted_iota(jnp.int32, lg.shape, 1)
    first = lambda hit: jnp.min(jnp.where(hit, lane, ROUTER_LANES), axis=-1, keepdims=True)
    gl = jnp.where(lane < N_GROUPS, lg, -jnp.inf)
    gmax = jnp.max(gl, axis=-1, keepdims=True)
    grp = first(gl == gmax)
    p_grp = 1.0 / jnp.sum(jnp.exp(gl - gmax), axis=-1, keepdims=True)
    lo = N_GROUPS + grp * EXPERTS_PER_GROUP
    el = jnp.where((lane >= lo) & (lane < lo + EXPERTS_PER_GROUP), lg, -jnp.inf)
    m1 = jnp.max(el, axis=-1, keepdims=True)
    i1 = first(el == m1)
    el = jnp.where(lane == i1, -jnp.inf, el)
    m2 = jnp.max(el, axis=-1, keepdims=True)
    i2 = first(el == m2)
    e2 = jnp.exp(m2 - m1)
    w1 = p_grp / (1.0 + e2)
    w2 = p_grp * e2 / (1.0 + e2)
    out = jnp.where(lane == 0, w1, jnp.where(lane == 1, w2, 0.0))
    out = jnp.where(lane == 2, (i1 - N_GROUPS).astype(F32), out)
    return jnp.where(lane == 3, (i2 - N_GROUPS).astype(F32), out)


def _xattn(h1, mk, mv, gx, wq, wo, gm, wr_hi, wr_lo, brt, tm=256):
    B, S, _ = h1.shape
    M = mk.shape[1]
    row = lambda b, i: (b, i, 0)
    c2 = lambda b, i: (0, 0)
    kv = pl.BlockSpec((None, M, D_MODEL), lambda b, i: (b, 0, 0))
    vec = pl.BlockSpec((1, D_MODEL), c2)
    wsq = pl.BlockSpec((D_MODEL, D_MODEL), c2)
    wr = pl.BlockSpec((D_MODEL, ROUTER_LANES), c2)
    return pl.pallas_call(
        _xattn_kernel,
        grid=(B, S // tm),
        in_specs=[pl.BlockSpec((None, tm, D_MODEL), row), kv, kv, vec, wsq, wsq, vec, wr, wr,
                  pl.BlockSpec((1, ROUTER_LANES), c2)],
        out_specs=[pl.BlockSpec((None, tm, D_MODEL), row), pl.BlockSpec((None, tm, D_MODEL), row),
                   pl.BlockSpec((None, tm, ROUTER_LANES), row)],
        out_shape=[jax.ShapeDtypeStruct((B, S, D_MODEL), F32), jax.ShapeDtypeStruct((B, S, D_MODEL), F32),
                   jax.ShapeDtypeStruct((B, S, ROUTER_LANES), F32)],
        compiler_params=_params(("parallel", "arbitrary")),
        name="xattn",
    )(h1, mk, mv, gx, wq, wo, gm, wr_hi, wr_lo, brt)


def _moe_kernel(be_ref, nrows_ref, src_ref, srcn_ref, dst_ref, u_hbm, wg_ref, wu_ref, wd_ref, o_hbm,
                xbuf, obuf, sem_in, sem_out):
    j = pl.program_id(0)
    nb = pl.num_programs(0)
    slot = j & 1
    n_cur = nrows_ref[j]

    def gather(idx_ref, s, wait):
        for r in range(MOE_ROWS):
            t = 0 if wait else idx_ref[0, 0, r]
            cp = pltpu.make_async_copy(u_hbm.at[pl.ds(t, 1), :], xbuf.at[s, pl.ds(r, 1), :], sem_in.at[s])
            cp.wait() if wait else cp.start()

    def scatter_row(s, r, wait):
        t = 0 if wait else dst_ref[0, 0, r]
        cp = pltpu.make_async_copy(obuf.at[s, pl.ds(r, 1), :], o_hbm.at[pl.ds(t, 1), :], sem_out.at[s])
        cp.wait() if wait else cp.start()

    def scatter(s, n, wait):
        @pl.when(n == MOE_ROWS)
        def _():
            for r in range(MOE_ROWS):
                scatter_row(s, r, wait)

        @pl.when((n > 0) & (n < MOE_ROWS))
        def _():
            def body(r, carry):
                scatter_row(s, r, wait)
                return carry
            lax.fori_loop(0, n, body, 0)

    @pl.when((j == 0) & (n_cur > 0))
    def _():
        gather(src_ref, 0, False)

    @pl.when((j + 1 < nb) & (nrows_ref[jnp.minimum(j + 1, nb - 1)] > 0))
    def _():
        gather(srcn_ref, 1 - slot, False)

    @pl.when(j >= 2)
    def _():
        scatter(slot, nrows_ref[jnp.maximum(j - 2, 0)], True)

    @pl.when(n_cur > 0)
    def _():
        gather(src_ref, slot, True)
        x = xbuf[slot].astype(BF16)
        g = _dot(x, wg_ref[...])
        u = _dot(x, wu_ref[...])
        hid = (g * _sigmoid(g)) * u
        obuf[slot] = _dot(hid.astype(BF16), wd_ref[...])
        scatter(slot, n_cur, False)

    @pl.when(j == nb - 1)
    def _():
        @pl.when(j >= 1)
        def _():
            scatter(1 - slot, nrows_ref[jnp.maximum(j - 1, 0)], True)
        scatter(slot, n_cur, True)


def _moe(block_e, n_rows, src, dst, u3, wg, wu, wd, n_out_rows):
    n_blocks = src.shape[0]
    idx = pl.BlockSpec((1, 1, MOE_ROWS), lambda j, be, nu: (j, 0, 0), memory_space=pltpu.SMEM)
    idx_next = pl.BlockSpec((1, 1, MOE_ROWS), lambda j, be, nu: (jnp.minimum(j + 1, n_blocks - 1), 0, 0),
                            memory_space=pltpu.SMEM)
    grid_spec = pltpu.PrefetchScalarGridSpec(
        num_scalar_prefetch=2,
        grid=(n_blocks,),
        in_specs=[idx, idx_next, idx,
                  pl.BlockSpec(memory_space=pl.ANY),
                  pl.BlockSpec((None, D_MODEL, D_EXPERT), lambda j, be, nu: (be[j], 0, 0)),
                  pl.BlockSpec((None, D_MODEL, D_EXPERT), lambda j, be, nu: (be[j], 0, 0)),
                  pl.BlockSpec((None, D_EXPERT, D_MODEL), lambda j, be, nu: (be[j], 0, 0))],
        out_specs=pl.BlockSpec(memory_space=pl.ANY),
        scratch_shapes=[pltpu.VMEM((2, MOE_ROWS, D_MODEL), F32), pltpu.VMEM((2, MOE_ROWS, D_MODEL), F32),
                        pltpu.SemaphoreType.DMA((2,)), pltpu.SemaphoreType.DMA((2,))])
    return pl.pallas_call(
        _moe_kernel,
        grid_spec=grid_spec,
        out_shape=jax.ShapeDtypeStruct((n_out_rows, D_MODEL), F32),
        compiler_params=_params(("arbitrary",)),
        name="moe",
    )(block_e, n_rows, src, src, dst, u3, wg, wu, wd)


def _combine_kernel(o0_ref, o1_ref, rt_ref, h_ref, g_ref, y_ref):
    y = o0_ref[...] * rt_ref[:, 0:1] + o1_ref[...] * rt_ref[:, 1:2]
    y_ref[...] = _rms(h_ref[...] + y, g_ref[...])


def _combine(o2, rt, h2, g, row0, n_rows, tm=512):
    T = h2.shape[0]
    first, k1 = row0 // tm, T // tm
    row = lambda i: (i + first, 0)
    return pl.pallas_call(
        _combine_kernel,
        grid=(n_rows // tm,),
        in_specs=[pl.BlockSpec((tm, D_MODEL), row), pl.BlockSpec((tm, D_MODEL), lambda i: (i + first + k1, 0)),
                  pl.BlockSpec((tm, ROUTER_LANES), row), pl.BlockSpec((tm, D_MODEL), row),
                  pl.BlockSpec((1, D_MODEL), lambda i: (0, 0))],
        out_specs=pl.BlockSpec((tm, D_MODEL), lambda i: (i, 0)),
        out_shape=jax.ShapeDtypeStruct((n_rows, D_MODEL), F32),
        compiler_params=_params(("parallel",)),
        name="combine",
    )(o2, o2, rt, h2, g)


def _t5_bucket(rel):
    half = N_BUCKETS // 2
    max_exact = half // 2
    n = jnp.abs(rel)
    large = max_exact + (jnp.log(jnp.maximum(n, 1).astype(jnp.float32) / max_exact)
                         / math.log(MAX_DISTANCE / max_exact) * (half - max_exact)).astype(jnp.int32)
    large = jnp.minimum(large, half - 1)
    return jnp.where(rel > 0, half, 0) + jnp.where(n < max_exact, n, large)


def _window_bias(rel_bias):
    r = jnp.arange(WINDOW)[:, None]
    c = jnp.arange(3 * WINDOW)[None, :]
    bias = rel_bias.astype(F32)[_t5_bucket(c - WINDOW - r)]
    return jnp.transpose(bias, (2, 0, 1))


def _split_cols(w, sizes):
    out, acc = [], 0
    for s in sizes:
        out.append(w[:, acc:acc + s])
        acc += s
    return out


def _dispatch_tables(eid, T):
    A = T * TOP_K
    assert A % MOE_ROWS == 0
    n_pad = N_EXPERTS * MOE_ROWS
    n_blocks = A // MOE_ROWS + N_EXPERTS
    id_bits = max(1, (A - 1).bit_length())
    assert id_bits + 7 < 31
    e_flat = eid.reshape(A)
    experts = jnp.arange(N_EXPERTS, dtype=jnp.int32)
    counts = jnp.sum((e_flat[:, None] == experts[None, :]).astype(jnp.int32), axis=0)
    padded = (counts + MOE_ROWS - 1) // MOE_ROWS * MOE_ROWS
    pend = jnp.cumsum(padded)
    pstart = pend - padded
    blk_first = jnp.arange(n_blocks, dtype=jnp.int32) * MOE_ROWS
    block_e = jnp.minimum(jnp.sum((pend[None, :] <= blk_first[:, None]).astype(jnp.int32), axis=1), N_EXPERTS - 1)
    n_rows = jnp.clip(counts[block_e] - (blk_first - pstart[block_e]), 0, MOE_ROWS).astype(jnp.int32)
    pad_e = jnp.repeat(experts, MOE_ROWS)
    pad_i = jnp.tile(jnp.arange(MOE_ROWS, dtype=jnp.int32), N_EXPERTS)
    pad_key = jnp.where(pad_i < (padded - counts)[pad_e], 2 * pad_e + 1, 2 * N_EXPERTS)
    words = jnp.concatenate([(2 * e_flat << id_bits) | jnp.arange(A, dtype=jnp.int32), pad_key << id_bits])
    asg = jnp.sort(words) & ((1 << id_bits) - 1)
    src = asg // TOP_K
    dst = (asg % TOP_K) * T + src
    return (block_e.astype(jnp.int32), n_rows,
            src.reshape(n_blocks, 1, MOE_ROWS), dst.reshape(n_blocks, 1, MOE_ROWS))


def kernel(x_prompt, x_sample, mem_prompt, mem_sample, rel_bias, norm_mix, w_in, attn_sink, gla_gate_up, gla_gate_bias, gla_norm, w_branch_a, w_branch_b, w_out, norm_xattn, norm_mem, w_xq, w_xkv, w_xo, norm_moe, w_router_group, b_router_group, w_router_expert, b_router_expert, w_expert_gate, w_expert_up, w_expert_down, norm_final):
    assert norm_mix.shape[0] == 1, "single-layer trunk"
    Bp, S, D = x_prompt.shape
    Bs = x_sample.shape[0]
    assert x_sample.shape[1] == S and D == D_MODEL
    B = Bp + Bs
    T = B * S
    l = 0

    xp = x_prompt.reshape(Bp * S, D)
    xs = x_sample.reshape(Bs * S, D)
    mem = jnp.concatenate([mem_prompt, mem_sample], axis=0)

    in_sizes = (N_HEADS_A * HEAD_DIM_A, N_KV_A * HEAD_DIM_A, N_KV_A * HEAD_DIM_A,
                N_HEADS_B * DK_B, N_HEADS_B * DK_B, N_HEADS_B * DV_B, N_HEADS_B * DV_B,
                2 * GATE_RANK, 2 * D_MODEL)
    w_aq, w_ak, w_av, w_bq, w_bk, w_bv, w_br, w_blr, w_g = _split_cols(w_in[l], in_sizes)
    w_blr = jnp.pad(w_blr, ((0, 0), (0, LANE - 2 * GATE_RANK)))
    w1 = jnp.concatenate([w_aq, w_ak, w_av, w_bq, w_bk, w_bv, w_br, w_g, w_blr], axis=1).astype(BF16)
    gu = gla_gate_up[l].reshape(2, GATE_RANK, N_HEADS_B, DK_B).transpose(0, 2, 1, 3)
    gu_pad = jnp.zeros((2, N_HEADS_B, LANE, DK_B), F32)
    gu_pad = gu_pad.at[0, :, 0:GATE_RANK].set(gu[0]).at[1, :, GATE_RANK:2 * GATE_RANK].set(gu[1]).astype(BF16)
    gb = gla_gate_bias[l].reshape(2, N_HEADS_B, 1, DK_B).astype(F32)
    gn = gla_norm[l].reshape(N_HEADS_B, 1, DV_B).astype(F32)
    bias = _window_bias(rel_bias)
    w_r = jnp.concatenate([w_router_group[l], w_router_expert[l]], axis=1).astype(F32)
    w_r = jnp.pad(w_r, ((0, 0), (0, ROUTER_LANES - w_r.shape[1])))
    w_r_hi = w_r.astype(BF16)
    w_r_lo = (w_r - w_r_hi.astype(F32)).astype(BF16)
    b_r = jnp.concatenate([b_router_group[l], b_router_expert[l]]).astype(F32)
    b_r = jnp.pad(b_r, (0, ROUTER_LANES - b_r.shape[0])).reshape(1, ROUTER_LANES)

    aq, ak, av, bq, bk, bv, br, gates, blr = _inproj(xp, xs, norm_mix[l].reshape(1, D), w1)
    r3 = lambda t: t.reshape(B, S, t.shape[-1])
    ao = _swa(r3(aq), r3(ak), r3(av), bias, attn_sink[l].astype(F32))
    ob = _gla(r3(bq), r3(bk), r3(bv), r3(br), r3(blr), gu_pad, gb, gn)
    h1 = _postmix(ao.reshape(T, D), ob.reshape(T, D), gates, xp, xs,
                  w_branch_a[l].astype(BF16), w_branch_b[l].astype(BF16), w_out[l].astype(BF16))

    mk, mv = _memkv(mem, norm_mem[l].reshape(1, D), w_xkv[l].astype(BF16))
    h2, u3, rt = _xattn(h1.reshape(B, S, D), mk, mv, norm_xattn[l].reshape(1, D),
                        w_xq[l].astype(BF16), w_xo[l].astype(BF16), norm_moe[l].reshape(1, D),
                        w_r_hi, w_r_lo, b_r)
    h2 = h2.reshape(T, D)
    u3 = u3.reshape(T, D)
    rt = rt.reshape(T, ROUTER_LANES)

    block_e, n_rows, src, dst = _dispatch_tables(rt[:, TOP_K:2 * TOP_K].astype(jnp.int32), T)
    o2 = _moe(block_e, n_rows, src, dst, u3,
              w_expert_gate[l].astype(BF16), w_expert_up[l].astype(BF16), w_expert_down[l].astype(BF16),
              T * TOP_K)
    g_final = norm_final.reshape(1, D)
    y_prompt = _combine(o2, rt, h2, g_final, 0, Bp * S)
    y_sample = _combine(o2, rt, h2, g_final, Bp * S, Bs * S)
    return (y_prompt.reshape(Bp, S, D), y_sample.reshape(Bs, S, D))
```

```python
import functools
import math

import jax
import jax.numpy as jnp
from jax import lax
from jax.experimental import pallas as pl
from jax.experimental.pallas import tpu as pltpu

F32 = jnp.float32
BF16 = jnp.bfloat16

D_MODEL = 1024
N_HEADS_A = 16
HEAD_DIM_A = 64
N_KV_A = 4
REP_A = N_HEADS_A // N_KV_A
WINDOW = 128
N_BUCKETS = 32
MAX_DISTANCE = 128
N_HEADS_B = 4
DK_B = 128
DV_B = 256
GATE_RANK = 16
GATE_TEMP = 16.0
GLA_CHUNK = 64
N_HEADS_X = 4
HEAD_DIM_X = D_MODEL // N_HEADS_X
N_GROUPS = 4
EXPERTS_PER_GROUP = 8
N_EXPERTS = N_GROUPS * EXPERTS_PER_GROUP
TOP_K = 2
D_EXPERT = D_MODEL // 2
MOE_ROWS = 256
ROUTER_LANES = 128

EPS = 1e-6
NEG_INF = -1e30

LANE = 128
VMEM_LIMIT = 48 * 1024 * 1024

SEG_WIDTHS = (N_KV_A * HEAD_DIM_A,
              N_HEADS_B * DK_B, N_HEADS_B * DK_B, N_HEADS_B * DV_B, N_HEADS_B * DV_B,
              2 * D_MODEL, LANE)
SEG_DTYPES = (BF16, BF16, BF16, BF16, BF16, BF16, F32)
BQ_SEG = 1
QT_ROWS = N_HEADS_A * HEAD_DIM_A
VT_ROWS = N_KV_A * HEAD_DIM_A


def _rms(x, g):
    return x * lax.rsqrt(jnp.mean(x * x, axis=-1, keepdims=True) + EPS) * g


def _sigmoid(x):
    return 1.0 / (1.0 + jnp.exp(-x))


def _dot(a, b):
    return jnp.dot(a, b, preferred_element_type=F32)


def _dot_nt(a, b):
    return lax.dot_general(a, b, (((1,), (1,)), ((), ())), preferred_element_type=F32)


def _dot_tn(a, b):
    return lax.dot_general(a, b, (((0,), (0,)), ((), ())), preferred_element_type=F32)


SUBLANES = D_MODEL // LANE
assert SUBLANES == 8, "a token row must fill exactly one f32 (8, 128) tile"


def _tile_rows_load(ref, n_rows, lead=()):
    return jnp.concatenate([ref[lead + (pl.ds(c, n_rows, stride=SUBLANES), slice(None))]
                            for c in range(SUBLANES)], axis=1)


def _tile_rows_store(ref, x, lead=()):
    for c in range(SUBLANES):
        ref[lead + (pl.ds(c, x.shape[0], stride=SUBLANES), slice(None))] = x[:, c * LANE:(c + 1) * LANE]


def _params(sem, vmem=VMEM_LIMIT):
    return pltpu.CompilerParams(dimension_semantics=sem, vmem_limit_bytes=vmem)


def _two_part_specs(n_first, n_second, tm):
    return [pl.BlockSpec((tm, D_MODEL), lambda i: (jnp.minimum(i, n_first - 1), 0)),
            pl.BlockSpec((tm, D_MODEL), lambda i: (jnp.clip(i - n_first, 0, n_second - 1), 0))]


def _two_part_rows(n_first, a_ref, b_ref):
    return jnp.where(pl.program_id(0) < n_first, a_ref[...], b_ref[...])


def _inproj_kernel(n_first, xa_ref, xb_ref, g_ref, w_ref, wt_ref, qt_ref, vt_ref, *out_refs):
    u = _rms(_two_part_rows(n_first, xa_ref, xb_ref), g_ref[...]).astype(BF16)
    t = _dot_nt(wt_ref[...], u)
    vt_ref[...] = t[QT_ROWS:, :].astype(vt_ref.dtype)
    for h in range(N_HEADS_A):
        g, rep = divmod(h, REP_A)
        for bl in range(u.shape[0] // WINDOW):
            lane0 = (bl * REP_A + rep) * WINDOW
            qt_ref[g, :, lane0:lane0 + WINDOW] = t[h * HEAD_DIM_A:(h + 1) * HEAD_DIM_A,
                                                    bl * WINDOW:(bl + 1) * WINDOW].astype(qt_ref.dtype)
    off = 0
    for seg, (o_ref, width) in enumerate(zip(out_refs, SEG_WIDTHS)):
        for c0 in range(0, width, 1024):
            cw = min(1024, width - c0)
            y = _dot(u, w_ref[:, off + c0:off + c0 + cw])
            if seg == BQ_SEG:
                y = y * (DK_B ** -0.5)
            o_ref[:, c0:c0 + cw] = y.astype(o_ref.dtype)
        off += width


def _inproj(xa, xb, g, w, wt, tm=256):
    na, nb = xa.shape[0] // tm, xb.shape[0] // tm
    T = (na + nb) * tm
    ncols = w.shape[1]
    out_shape = [jax.ShapeDtypeStruct((N_KV_A, HEAD_DIM_A, REP_A * T), BF16),
                 jax.ShapeDtypeStruct((VT_ROWS, T), BF16)]
    out_shape += [jax.ShapeDtypeStruct((T, wd), dt) for wd, dt in zip(SEG_WIDTHS, SEG_DTYPES)]
    out_specs = [pl.BlockSpec((N_KV_A, HEAD_DIM_A, REP_A * tm), lambda i: (0, 0, i)),
                 pl.BlockSpec((VT_ROWS, tm), lambda i: (0, i))]
    out_specs += [pl.BlockSpec((tm, wd), lambda i: (i, 0)) for wd in SEG_WIDTHS]
    return pl.pallas_call(
        functools.partial(_inproj_kernel, na),
        grid=(na + nb,),
        in_specs=_two_part_specs(na, nb, tm) +
                 [pl.BlockSpec((1, D_MODEL), lambda i: (0, 0)),
                  pl.BlockSpec((D_MODEL, ncols), lambda i: (0, 0), pipeline_mode=pl.Buffered(1)),
                  pl.BlockSpec((QT_ROWS + VT_ROWS, D_MODEL), lambda i: (0, 0), pipeline_mode=pl.Buffered(1))],
        out_specs=out_specs,
        out_shape=out_shape,
        compiler_params=_params(("parallel",)),
        name="inproj",
    )(xa, xb, g, w, wt)


def _swa_kernel(qt_ref, kp_ref, kc_ref, kn_ref, vp_ref, vc_ref, vn_ref, bias_ref, band_ref, sink_ref,
                o_ref, ot_scr):
    i = pl.program_id(1)
    nb = pl.num_programs(1)
    tbl = (i == 0).astype(jnp.int32) + 2 * (i == nb - 1).astype(jnp.int32)
    kcat = jnp.concatenate([kp_ref[...], kc_ref[...], kn_ref[...]], axis=0)
    vcat = jnp.concatenate([vp_ref[...], vc_ref[...], vn_ref[...]], axis=1)
    for g in range(N_KV_A):
        gsl = slice(g * HEAD_DIM_A, (g + 1) * HEAD_DIM_A)
        s = _dot(kcat[:, gsl], qt_ref[g]) + bias_ref[g]
        s = jnp.where(band_ref[tbl] != 0.0, s, NEG_INF)
        sink = sink_ref[g]
        m = jnp.maximum(jnp.max(s, axis=0, keepdims=True), sink)
        p = jnp.exp(s - m)
        denom = jnp.sum(p, axis=0, keepdims=True) + jnp.exp(sink - m)
        ot_scr[gsl, :] = _dot(vcat[gsl, :], p.astype(BF16)) / denom
    for rep in range(REP_A):
        blk = ot_scr[:, rep * WINDOW:(rep + 1) * WINDOW]
        o_ref[:, rep * VT_ROWS:(rep + 1) * VT_ROWS] = blk.T.astype(o_ref.dtype)


def _swa(qt, k, vt, bias_t, band, sink_rows, B, S):
    nb = S // WINDOW
    kvw = N_KV_A * HEAD_DIM_A
    rw = REP_A * WINDOW

    def clampi(b, j):
        return b * nb + jnp.clip(j, 0, nb - 1)

    k_specs = [pl.BlockSpec((WINDOW, kvw), lambda b, i, d=d: (clampi(b, i + d), 0)) for d in (-1, 0, 1)]
    v_specs = [pl.BlockSpec((kvw, WINDOW), lambda b, i, d=d: (0, clampi(b, i + d))) for d in (-1, 0, 1)]
    const3 = lambda b, i: (0, 0, 0)
    return pl.pallas_call(
        _swa_kernel,
        grid=(B, nb),
        in_specs=[pl.BlockSpec((N_KV_A, HEAD_DIM_A, rw), lambda b, i: (0, 0, b * nb + i))] + k_specs + v_specs +
                 [pl.BlockSpec((N_KV_A, 3 * WINDOW, rw), const3),
                  pl.BlockSpec((4, 3 * WINDOW, rw), const3),
                  pl.BlockSpec((N_KV_A, 1, rw), const3)],
        out_specs=pl.BlockSpec((WINDOW, D_MODEL), lambda b, i: (b * nb + i, 0)),
        out_shape=jax.ShapeDtypeStruct((B * S, D_MODEL), BF16),
        scratch_shapes=[pltpu.VMEM((kvw, rw), F32)],
        compiler_params=_params(("parallel", "arbitrary")),
        name="swa",
    )(qt, k, k, k, vt, vt, vt, bias_t, band, sink_rows)


GLA_ROWS = 512
GLA_UNROLL = 8


def _chunk_scan(g, pos, fwd):
    n = g.shape[0]
    s = 1
    while s < GLA_CHUNK:
        if fwd:
            g = g + jnp.where(pos >= s, pltpu.roll(g, s, axis=0), 0.0)
        else:
            g = g + jnp.where(pos < GLA_CHUNK - s, pltpu.roll(g, n - s, axis=0), 0.0)
        s *= 2
    return g


def _gla_kernel(q_ref, k_ref, v_ref, br_ref, blr_ref, gu_ref, gb_ref, gn_ref, o_ref,
                qd_scr, a_scr, kv_scr, dl_scr, acc_scr, st_scr):
    S = q_ref.shape[0]
    C = GLA_CHUNK
    nc = S // C
    R = GLA_ROWS
    cpr = R // C
    ri = lax.broadcasted_iota(jnp.int32, (C, C), 0)
    ci = lax.broadcasted_iota(jnp.int32, (C, C), 1)
    pos = lax.broadcasted_iota(jnp.int32, (R, DK_B), 0) & (C - 1)

    for d in range(2):
        fwd = d == 0

        def prep(t, carry):
            sl = pl.ds(pl.multiple_of(t * R, R), R)
            z = _dot(blr_ref[sl, :].astype(BF16), gu_ref[d]) + gb_ref[d]
            g = (jnp.minimum(z, 0.0) - jnp.log(1.0 + jnp.exp(-jnp.abs(z)))) * (1.0 / GATE_TEMP)
            b = _chunk_scan(g, pos, fwd).reshape(cpr, C, DK_B)
            b_end = b[:, C - 1:C, :] if fwd else b[:, 0:1, :]
            q = q_ref[sl, :].astype(F32).reshape(cpr, C, DK_B)
            k = k_ref[sl, :].astype(F32).reshape(cpr, C, DK_B)
            qd = (q * jnp.exp(b)).astype(BF16)
            ki = (k * jnp.exp(-b)).astype(BF16)
            ks = (k * jnp.exp(b_end - b)).astype(BF16)
            qd_scr[sl, :] = qd.reshape(R, DK_B)
            dl_scr[pl.ds(pl.multiple_of(t * cpr, cpr), cpr)] = jnp.broadcast_to(jnp.exp(b_end), (cpr, 8, DK_B))
            for c in range(cpr):
                csl = pl.ds(pl.multiple_of(t * R + c * C, C), C)
                a = _dot_nt(qd[c], ki[c])
                a_scr[csl, :] = jnp.where((ri >= ci) if fwd else (ri < ci), a, 0.0).astype(BF16)
                kv_scr[t * cpr + c] = _dot_tn(v_ref[csl, :], ks[c])
            return carry

        lax.fori_loop(0, S // R, prep, 0)
        st_scr[...] = jnp.zeros_like(st_scr)

        def chunk(t, carry):
            n = t if fwd else nc - 1 - t
            sl = pl.ds(pl.multiple_of(n * C, C), C)
            st = st_scr[...]
            o = _dot(a_scr[sl, :], v_ref[sl, :]) + _dot_nt(qd_scr[sl, :], st.astype(BF16))
            st_scr[...] = st * dl_scr[n][0:1, :] + kv_scr[n]
            if fwd:
                acc_scr[sl, :] = o
            else:
                o = o + acc_scr[sl, :]
                o = o * lax.rsqrt(jnp.mean(o * o, axis=-1, keepdims=True) + EPS) * gn_ref[...]
                r = br_ref[sl, :].astype(F32)
                o_ref[sl, :] = (o * (r * _sigmoid(r))).astype(o_ref.dtype)
            return carry

        lax.fori_loop(0, nc, chunk, 0, unroll=GLA_UNROLL)


def _gla(bq, bk, bv, br, blr, gu, gb, gn):
    B, S, _ = bq.shape
    nc = S // GLA_CHUNK
    return pl.pallas_call(
        _gla_kernel,
        grid=(B, N_HEADS_B),
        in_specs=[pl.BlockSpec((None, S, DK_B), lambda b, h: (b, 0, h)),
                  pl.BlockSpec((None, S, DK_B), lambda b, h: (b, 0, h)),
                  pl.BlockSpec((None, S, DV_B), lambda b, h: (b, 0, h)),
                  pl.BlockSpec((None, S, DV_B), lambda b, h: (b, 0, h)),
                  pl.BlockSpec((None, S, LANE), lambda b, h: (b, 0, 0)),
                  pl.BlockSpec((2, None, LANE, DK_B), lambda b, h: (0, h, 0, 0)),
                  pl.BlockSpec((2, None, 1, DK_B), lambda b, h: (0, h, 0, 0)),
                  pl.BlockSpec((None, 1, DV_B), lambda b, h: (h, 0, 0))],
        out_specs=pl.BlockSpec((None, S, DV_B), lambda b, h: (b, 0, h)),
        out_shape=jax.ShapeDtypeStruct((B, S, N_HEADS_B * DV_B), BF16),
        scratch_shapes=[pltpu.VMEM((S, DK_B), BF16), pltpu.VMEM((S, GLA_CHUNK), BF16),
                        pltpu.VMEM((nc, DV_B, DK_B), F32),
                        pltpu.VMEM((nc, 8, DK_B), F32),
                        pltpu.VMEM((S, DV_B), F32),
                        pltpu.VMEM((DV_B, DK_B), F32)],
        compiler_params=_params(("parallel", "arbitrary")),
        name="gla",
    )(bq, bk, bv, br, blr, gu, gb, gn)


def _postmix_kernel(n_first, ao_ref, ob_ref, gates_ref, xa_ref, xb_ref, wa_ref, wb_ref, wo_ref, h_ref):
    ya = _dot(ao_ref[...], wa_ref[...])
    yb = _dot(ob_ref[...], wb_ref[...])
    ga = gates_ref[:, :D_MODEL].astype(F32)
    gb = gates_ref[:, D_MODEL:].astype(F32)
    merged = _sigmoid(ga) * ya + _sigmoid(gb) * yb
    h_ref[...] = _two_part_rows(n_first, xa_ref, xb_ref) + _dot(merged.astype(BF16), wo_ref[...])


def _postmix(ao, ob, gates, xa, xb, wa, wb, wo, tm=256):
    na, nb = xa.shape[0] // tm, xb.shape[0] // tm
    T = (na + nb) * tm
    row = lambda i: (i, 0)
    const = lambda i: (0, 0)
    wspec = pl.BlockSpec((D_MODEL, D_MODEL), const)
    return pl.pallas_call(
        functools.partial(_postmix_kernel, na),
        grid=(na + nb,),
        in_specs=[pl.BlockSpec((tm, D_MODEL), row), pl.BlockSpec((tm, D_MODEL), row),
                  pl.BlockSpec((tm, 2 * D_MODEL), row)] + _two_part_specs(na, nb, tm) +
                 [wspec, wspec, wspec],
        out_specs=pl.BlockSpec((tm, D_MODEL), row),
        out_shape=jax.ShapeDtypeStruct((T, D_MODEL), F32),
        compiler_params=_params(("parallel",)),
        name="postmix",
    )(ao, ob, gates, xa, xb, wa, wb, wo)


def _memkv_kernel(m_ref, g_ref, w_ref, k_ref, v_ref):
    u = _rms(m_ref[...], g_ref[...]).astype(BF16)
    k_ref[...] = _dot(u, w_ref[:, :D_MODEL]).astype(k_ref.dtype)
    v_ref[...] = _dot(u, w_ref[:, D_MODEL:]).astype(v_ref.dtype)


def _memkv(mem, g, w):
    B, M, _ = mem.shape
    blk = pl.BlockSpec((None, M, D_MODEL), lambda b: (b, 0, 0))
    return pl.pallas_call(
        _memkv_kernel,
        grid=(B,),
        in_specs=[blk, pl.BlockSpec((1, D_MODEL), lambda b: (0, 0)),
                  pl.BlockSpec((D_MODEL, 2 * D_MODEL), lambda b: (0, 0))],
        out_specs=[blk, blk],
        out_shape=[jax.ShapeDtypeStruct((B, M, D_MODEL), BF16)] * 2,
        compiler_params=_params(("parallel",)),
        name="memkv",
    )(mem, g, w)


def _xattn_kernel(h_ref, k_ref, v_ref, gx_ref, wq_ref, wo_ref, gm_ref, wr_hi_ref, wr_lo_ref, br_ref,
                  h2_ref, u3_ref, rt_ref):
    h1 = h_ref[...]
    u2 = _rms(h1, gx_ref[...]).astype(BF16)
    qx = _dot(u2, wq_ref[...]).astype(BF16)
    scale = HEAD_DIM_X ** -0.5
    outs = []
    for hh in range(N_HEADS_X):
        sl = slice(hh * HEAD_DIM_X, (hh + 1) * HEAD_DIM_X)
        s = _dot_nt(qx[:, sl], k_ref[:, sl]) * scale
        m = jnp.max(s, axis=-1, keepdims=True)
        p = jnp.exp(s - m)
        p = p / jnp.sum(p, axis=-1, keepdims=True)
        outs.append(_dot(p.astype(BF16), v_ref[:, sl]).astype(BF16))
    o = jnp.concatenate(outs, axis=-1)
    h2 = h1 + _dot(o, wo_ref[...])
    h2_ref[...] = h2
    u3 = _rms(h2, gm_ref[...])
    _tile_rows_store(u3_ref, u3)
    u_hi = u3.astype(BF16)
    u_lo = (u3 - u_hi.astype(F32)).astype(BF16)
    lg = _dot(u_hi, wr_hi_ref[...]) + (_dot(u_hi, wr_lo_ref[...]) + _dot(u_lo, wr_hi_ref[...]))
    rt_ref[...] = _route_lanes(lg + br_ref[...])


def _route_lanes(lg):
    lane = lax.broadcasted_iota(jnp.int32, lg.shape, 1)
    first = lambda hit: jnp.min(jnp.where(hit, lane, ROUTER_LANES), axis=-1, keepdims=True)
    gl = jnp.where(lane < N_GROUPS, lg, -jnp.inf)
    gmax = jnp.max(gl, axis=-1, keepdims=True)
    grp = first(gl == gmax)
    p_grp = 1.0 / jnp.sum(jnp.exp(gl - gmax), axis=-1, keepdims=True)
    lo = N_GROUPS + grp * EXPERTS_PER_GROUP
    el = jnp.where((lane >= lo) & (lane < lo + EXPERTS_PER_GROUP), lg, -jnp.inf)
    m1 = jnp.max(el, axis=-1, keepdims=True)
    i1 = first(el == m1)
    el = jnp.where(lane == i1, -jnp.inf, el)
    m2 = jnp.max(el, axis=-1, keepdims=True)
    i2 = first(el == m2)
    e2 = jnp.exp(m2 - m1)
    w1 = p_grp / (1.0 + e2)
    w2 = p_grp * e2 / (1.0 + e2)
    out = jnp.where(lane == 0, w1, jnp.where(lane == 1, w2, 0.0))
    out = jnp.where(lane == 2, (i1 - N_GROUPS).astype(F32), out)
    return jnp.where(lane == 3, (i2 - N_GROUPS).astype(F32), out)


def _xattn(h1, mk, mv, gx, wq, wo, gm, wr_hi, wr_lo, brt, tm=256):
    B, S, _ = h1.shape
    M = mk.shape[1]
    row = lambda b, i: (b, i, 0)
    flat = lambda b, i: (b * (S // tm) + i, 0)
    c2 = lambda b, i: (0, 0)
    kv = pl.BlockSpec((None, M, D_MODEL), lambda b, i: (b, 0, 0))
    vec = pl.BlockSpec((1, D_MODEL), c2)
    wsq = pl.BlockSpec((D_MODEL, D_MODEL), c2)
    wr = pl.BlockSpec((D_MODEL, ROUTER_LANES), c2)
    return pl.pallas_call(
        _xattn_kernel,
        grid=(B, S // tm),
        in_specs=[pl.BlockSpec((None, tm, D_MODEL), row), kv, kv, vec, wsq, wsq, vec, wr, wr,
                  pl.BlockSpec((1, ROUTER_LANES), c2)],
        out_specs=[pl.BlockSpec((tm, D_MODEL), flat), pl.BlockSpec((tm * SUBLANES, LANE), flat),
                   pl.BlockSpec((tm, ROUTER_LANES), flat)],
        out_shape=[jax.ShapeDtypeStruct((B * S, D_MODEL), F32), jax.ShapeDtypeStruct((B * S * SUBLANES, LANE), F32),
                   jax.ShapeDtypeStruct((B * S, ROUTER_LANES), F32)],
        compiler_params=_params(("parallel", "arbitrary")),
        name="xattn",
    )(h1, mk, mv, gx, wq, wo, gm, wr_hi, wr_lo, brt)


def _moe_kernel(be_ref, nrows_ref, src_ref, srcn_ref, dst_ref, u_hbm, wg_ref, wu_ref, wd_ref, o_hbm,
                xbuf, obuf, sem_in, sem_out):
    j = pl.program_id(0)
    nb = pl.num_programs(0)
    slot = j & 1
    n_cur = nrows_ref[j]

    def tile(r):
        return pl.ds(r * SUBLANES, SUBLANES)

    def gather_row(idx_ref, s, r, wait):
        t = 0 if wait else pl.multiple_of(idx_ref[0, 0, r], SUBLANES)
        cp = pltpu.make_async_copy(u_hbm.at[pl.ds(t, SUBLANES), :], xbuf.at[s, tile(r), :], sem_in.at[s])
        cp.wait() if wait else cp.start()

    def scatter_row(s, r, wait):
        t = 0 if wait else pl.multiple_of(dst_ref[0, 0, r], SUBLANES)
        cp = pltpu.make_async_copy(obuf.at[s, tile(r), :], o_hbm.at[pl.ds(t, SUBLANES), :], sem_out.at[s])
        cp.wait() if wait else cp.start()

    def scatter(s, n, wait):
        @pl.when(n == MOE_ROWS)
        def _():
            for r in range(MOE_ROWS):
                scatter_row(s, r, wait)

        @pl.when((n > 0) & (n < MOE_ROWS))
        def _():
            def body(r, carry):
                scatter_row(s, r, wait)
                return carry
            lax.fori_loop(0, n, body, 0)

    @pl.when(j == 0)
    def _():
        for r in range(MOE_ROWS):
            gather_row(src_ref, 0, r, False)

    @pl.when(j >= 2)
    def _():
        scatter(slot, nrows_ref[jnp.maximum(j - 2, 0)], True)

    for r in range(MOE_ROWS):
        gather_row(None, slot, r, True)
    x = _tile_rows_load(xbuf, MOE_ROWS, (slot,)).astype(BF16)
    issue = [lambda r=r: gather_row(srcn_ref, 1 - slot, r, False) for r in range(MOE_ROWS)]
    for f in issue[:MOE_ROWS // 2]:
        f()
    g = _dot(x, wg_ref[...])
    for f in issue[MOE_ROWS // 2:]:
        f()
    u = _dot(x, wu_ref[...])
    hid = (g * _sigmoid(g)) * u
    _tile_rows_store(obuf, _dot(hid.astype(BF16), wd_ref[...]), (slot,))
    scatter(slot, n_cur, False)

    @pl.when(j == nb - 1)
    def _():
        for r in range(MOE_ROWS):
            gather_row(None, 1 - slot, r, True)

        @pl.when(j >= 1)
        def _():
            scatter(1 - slot, nrows_ref[jnp.maximum(j - 1, 0)], True)
        scatter(slot, n_cur, True)


def _moe(block_e, n_rows, src, dst, u3, wg, wu, wd, n_out_rows):
    n_blocks = src.shape[0]
    idx = pl.BlockSpec((1, 1, MOE_ROWS), lambda j, be, nu: (j, 0, 0), memory_space=pltpu.SMEM)
    idx_next = pl.BlockSpec((1, 1, MOE_ROWS), lambda j, be, nu: (jnp.minimum(j + 1, n_blocks - 1), 0, 0),
                            memory_space=pltpu.SMEM)
    grid_spec = pltpu.PrefetchScalarGridSpec(
        num_scalar_prefetch=2,
        grid=(n_blocks,),
        in_specs=[idx, idx_next, idx,
                  pl.BlockSpec(memory_space=pl.ANY),
                  pl.BlockSpec((None, D_MODEL, D_EXPERT), lambda j, be, nu: (be[j], 0, 0)),
                  pl.BlockSpec((None, D_MODEL, D_EXPERT), lambda j, be, nu: (be[j], 0, 0)),
                  pl.BlockSpec((None, D_EXPERT, D_MODEL), lambda j, be, nu: (be[j], 0, 0))],
        out_specs=pl.BlockSpec(memory_space=pl.ANY),
        scratch_shapes=[pltpu.VMEM((2, MOE_ROWS * SUBLANES, LANE), F32),
                        pltpu.VMEM((2, MOE_ROWS * SUBLANES, LANE), F32),
                        pltpu.SemaphoreType.DMA((2,)), pltpu.SemaphoreType.DMA((2,))])
    return pl.pallas_call(
        _moe_kernel,
        grid_spec=grid_spec,
        out_shape=jax.ShapeDtypeStruct((n_out_rows * SUBLANES, LANE), F32),
        compiler_params=_params(("arbitrary",)),
        name="moe",
    )(block_e, n_rows, src, src, dst, u3, wg, wu, wd)


def _combine_kernel(o0_ref, o1_ref, rt_ref, h_ref, g_ref, y_ref):
    n = h_ref.shape[0]
    y = _tile_rows_load(o0_ref, n) * rt_ref[:, 0:1] + _tile_rows_load(o1_ref, n) * rt_ref[:, 1:2]
    y_ref[...] = _rms(h_ref[...] + y, g_ref[...])


def _combine(o2, rt, h2, g, row0, n_rows, tm=512):
    T = h2.shape[0]
    first, k1 = row0 // tm, T // tm
    row = lambda i: (i + first, 0)
    return pl.pallas_call(
        _combine_kernel,
        grid=(n_rows // tm,),
        in_specs=[pl.BlockSpec((tm * SUBLANES, LANE), row),
                  pl.BlockSpec((tm * SUBLANES, LANE), lambda i: (i + first + k1, 0)),
                  pl.BlockSpec((tm, ROUTER_LANES), row), pl.BlockSpec((tm, D_MODEL), row),
                  pl.BlockSpec((1, D_MODEL), lambda i: (0, 0))],
        out_specs=pl.BlockSpec((tm, D_MODEL), lambda i: (i, 0)),
        out_shape=jax.ShapeDtypeStruct((n_rows, D_MODEL), F32),
        compiler_params=_params(("parallel",)),
        name="combine",
    )(o2, o2, rt, h2, g)


def _t5_bucket(rel):
    half = N_BUCKETS // 2
    max_exact = half // 2
    n = jnp.abs(rel)
    large = max_exact + (jnp.log(jnp.maximum(n, 1).astype(jnp.float32) / max_exact)
                         / math.log(MAX_DISTANCE / max_exact) * (half - max_exact)).astype(jnp.int32)
    large = jnp.minimum(large, half - 1)
    return jnp.where(rel > 0, half, 0) + jnp.where(n < max_exact, n, large)


def _window_tables(rel_bias, attn_sink):
    c = jnp.arange(3 * WINDOW)[:, None]
    r = jnp.arange(WINDOW)[None, :]
    bucket = _t5_bucket(c - WINDOW - r)
    rb = rel_bias.astype(F32)
    bias = jnp.zeros((N_HEADS_A, 3 * WINDOW, WINDOW), F32)
    for b in range(N_BUCKETS):
        bias = jnp.where((bucket == b)[None], rb[b][:, None, None], bias)
    bias_t = bias.reshape(N_KV_A, REP_A, 3 * WINDOW, WINDOW).transpose(0, 2, 1, 3)
    bias_t = bias_t.reshape(N_KV_A, 3 * WINDOW, REP_A * WINDOW)
    in_band = jnp.tile(jnp.abs(c - WINDOW - r) <= WINDOW, (1, REP_A))
    has_prev = c >= WINDOW
    has_next = c < 2 * WINDOW
    band = jnp.stack([in_band, in_band & has_prev, in_band & has_next, in_band & has_prev & has_next])
    sink_rows = jnp.repeat(attn_sink.astype(F32).reshape(N_KV_A, REP_A), WINDOW, axis=1)
    return bias_t, band.astype(F32), sink_rows.reshape(N_KV_A, 1, REP_A * WINDOW)


def _split_cols(w, sizes):
    out, acc = [], 0
    for s in sizes:
        out.append(w[:, acc:acc + s])
        acc += s
    return out


def _dispatch_tables(eid, T):
    A = T * TOP_K
    assert A % MOE_ROWS == 0
    n_pad = N_EXPERTS * MOE_ROWS
    n_blocks = A // MOE_ROWS + N_EXPERTS
    id_bits = max(1, (A - 1).bit_length())
    assert id_bits + 7 < 31
    e_flat = eid.reshape(A)
    experts = jnp.arange(N_EXPERTS, dtype=jnp.int32)
    counts = jnp.sum((e_flat[:, None] == experts[None, :]).astype(jnp.int32), axis=0)
    padded = (counts + MOE_ROWS - 1) // MOE_ROWS * MOE_ROWS
    pend = jnp.cumsum(padded)
    pstart = pend - padded
    blk_first = jnp.arange(n_blocks, dtype=jnp.int32) * MOE_ROWS
    block_e = jnp.minimum(jnp.sum((pend[None, :] <= blk_first[:, None]).astype(jnp.int32), axis=1), N_EXPERTS - 1)
    n_rows = jnp.clip(counts[block_e] - (blk_first - pstart[block_e]), 0, MOE_ROWS).astype(jnp.int32)
    pad_e = jnp.repeat(experts, MOE_ROWS)
    pad_i = jnp.tile(jnp.arange(MOE_ROWS, dtype=jnp.int32), N_EXPERTS)
    pad_key = jnp.where(pad_i < (padded - counts)[pad_e], 2 * pad_e + 1, 2 * N_EXPERTS)
    words = jnp.concatenate([(2 * e_flat << id_bits) | jnp.arange(A, dtype=jnp.int32), pad_key << id_bits])
    asg = jnp.sort(words) & ((1 << id_bits) - 1)
    src = asg // TOP_K
    dst = (asg % TOP_K) * T + src
    return (block_e.astype(jnp.int32), n_rows,
            (src * SUBLANES).reshape(n_blocks, 1, MOE_ROWS), (dst * SUBLANES).reshape(n_blocks, 1, MOE_ROWS))


def kernel(x_prompt, x_sample, mem_prompt, mem_sample, rel_bias, norm_mix, w_in, attn_sink, gla_gate_up, gla_gate_bias, gla_norm, w_branch_a, w_branch_b, w_out, norm_xattn, norm_mem, w_xq, w_xkv, w_xo, norm_moe, w_router_group, b_router_group, w_router_expert, b_router_expert, w_expert_gate, w_expert_up, w_expert_down, norm_final):
    assert norm_mix.shape[0] == 1, "single-layer trunk"
    Bp, S, D = x_prompt.shape
    Bs = x_sample.shape[0]
    assert x_sample.shape[1] == S and D == D_MODEL
    B = Bp + Bs
    T = B * S
    l = 0

    xp = x_prompt.reshape(Bp * S, D)
    xs = x_sample.reshape(Bs * S, D)
    mem = jnp.concatenate([mem_prompt, mem_sample], axis=0)

    in_sizes = (N_HEADS_A * HEAD_DIM_A, N_KV_A * HEAD_DIM_A, N_KV_A * HEAD_DIM_A,
                N_HEADS_B * DK_B, N_HEADS_B * DK_B, N_HEADS_B * DV_B, N_HEADS_B * DV_B,
                2 * GATE_RANK, 2 * D_MODEL)
    w_aq, w_ak, w_av, w_bq, w_bk, w_bv, w_br, w_blr, w_g = _split_cols(w_in[l], in_sizes)
    w_blr = jnp.pad(w_blr, ((0, 0), (0, LANE - 2 * GATE_RANK)))
    w1 = jnp.concatenate([w_ak, w_bq, w_bk, w_bv, w_br, w_g, w_blr], axis=1).astype(BF16)
    assert math.log2(HEAD_DIM_A) % 2 == 0
    wt = jnp.concatenate([w_aq.T * (HEAD_DIM_A ** -0.5), w_av.T], axis=0).astype(BF16)
    wa = w_branch_a[l].reshape(N_KV_A, REP_A, HEAD_DIM_A, D).transpose(1, 0, 2, 3).reshape(D, D).astype(BF16)
    gu = gla_gate_up[l].reshape(2, GATE_RANK, N_HEADS_B, DK_B).transpose(0, 2, 1, 3)
    gu_pad = jnp.zeros((2, N_HEADS_B, LANE, DK_B), F32)
    gu_pad = gu_pad.at[0, :, 0:GATE_RANK].set(gu[0]).at[1, :, GATE_RANK:2 * GATE_RANK].set(gu[1]).astype(BF16)
    gb = gla_gate_bias[l].reshape(2, N_HEADS_B, 1, DK_B).astype(F32)
    gn = gla_norm[l].reshape(N_HEADS_B, 1, DV_B).astype(F32)
    bias_t, band, sink_rows = _window_tables(rel_bias, attn_sink[l])
    w_r = jnp.concatenate([w_router_group[l], w_router_expert[l]], axis=1).astype(F32)
    w_r = jnp.pad(w_r, ((0, 0), (0, ROUTER_LANES - w_r.shape[1])))
    w_r_hi = w_r.astype(BF16)
    w_r_lo = (w_r - w_r_hi.astype(F32)).astype(BF16)
    b_r = jnp.concatenate([b_router_group[l], b_router_expert[l]]).astype(F32)
    b_r = jnp.pad(b_r, (0, ROUTER_LANES - b_r.shape[0])).reshape(1, ROUTER_LANES)

    qt, vt, ak, bq, bk, bv, br, gates, blr = _inproj(xp, xs, norm_mix[l].reshape(1, D), w1, wt)
    r3 = lambda t: t.reshape(B, S, t.shape[-1])
    ao = _swa(qt, ak, vt, bias_t, band, sink_rows, B, S)
    ob = _gla(r3(bq), r3(bk), r3(bv), r3(br), r3(blr), gu_pad, gb, gn)
    h1 = _postmix(ao, ob.reshape(T, D), gates, xp, xs,
                  wa, w_branch_b[l].astype(BF16), w_out[l].astype(BF16))

    mk, mv = _memkv(mem, norm_mem[l].reshape(1, D), w_xkv[l].astype(BF16))
    h2, u3, rt = _xattn(h1.reshape(B, S, D), mk, mv, norm_xattn[l].reshape(1, D),
                        w_xq[l].astype(BF16), w_xo[l].astype(BF16), norm_moe[l].reshape(1, D),
                        w_r_hi, w_r_lo, b_r)

    block_e, n_rows, src, dst = _dispatch_tables(rt[:, TOP_K:2 * TOP_K].astype(jnp.int32), T)
    o2 = _moe(block_e, n_rows, src, dst, u3,
              w_expert_gate[l].astype(BF16), w_expert_up[l].astype(BF16), w_expert_down[l].astype(BF16),
              T * TOP_K)
    g_final = norm_final.reshape(1, D)
    y_prompt = _combine(o2, rt, h2, g_final, 0, Bp * S)
    y_sample = _combine(o2, rt, h2, g_final, Bp * S, Bs * S)
    return (y_prompt.reshape(Bp, S, D), y_sample.reshape(Bs, S, D))
```

```python
import functools
import math

import jax
import jax.numpy as jnp
from jax import lax
from jax.experimental import pallas as pl
from jax.experimental.pallas import tpu as pltpu

F32 = jnp.float32
BF16 = jnp.bfloat16

D_MODEL = 1024
N_HEADS_A = 16
HEAD_DIM_A = 64
N_KV_A = 4
REP_A = N_HEADS_A // N_KV_A
WINDOW = 128
N_BUCKETS = 32
MAX_DISTANCE = 128
N_HEADS_B = 4
DK_B = 128
DV_B = 256
GATE_RANK = 16
GATE_TEMP = 16.0
GLA_CHUNK = 64
N_HEADS_X = 4
HEAD_DIM_X = D_MODEL // N_HEADS_X
N_GROUPS = 4
EXPERTS_PER_GROUP = 8
N_EXPERTS = N_GROUPS * EXPERTS_PER_GROUP
TOP_K = 2
D_EXPERT = D_MODEL // 2
MOE_ROWS = 256
ROUTER_LANES = 128

EPS = 1e-6
NEG_INF = -1e30

LANE = 128
VMEM_LIMIT = 48 * 1024 * 1024
ROW_TILE_INPROJ = 512
ROW_TILE_POSTMIX = 512
ROW_TILE_XATTN = 1024
ROW_TILE_COMBINE = 512

SEG_WIDTHS = (N_KV_A * HEAD_DIM_A,
              N_HEADS_B * DK_B, N_HEADS_B * DK_B, N_HEADS_B * DV_B, N_HEADS_B * DV_B,
              2 * D_MODEL, LANE)
SEG_DTYPES = (BF16, BF16, BF16, BF16, BF16, BF16, F32)
BQ_SEG = 1
QT_ROWS = N_HEADS_A * HEAD_DIM_A
VT_ROWS = N_KV_A * HEAD_DIM_A


def _rms(x, g):
    return x * lax.rsqrt(jnp.mean(x * x, axis=-1, keepdims=True) + EPS) * g


def _sigmoid(x):
    return 1.0 / (1.0 + jnp.exp(-x))


def _dot(a, b):
    return jnp.dot(a, b, preferred_element_type=F32)


def _dot_nt(a, b):
    return lax.dot_general(a, b, (((1,), (1,)), ((), ())), preferred_element_type=F32)


def _dot_tn(a, b):
    return lax.dot_general(a, b, (((0,), (0,)), ((), ())), preferred_element_type=F32)


SUBLANES = D_MODEL // LANE
assert SUBLANES == 8, "a token row must fill exactly one f32 (8, 128) tile"


def _tile_rows_load(ref, n_rows, lead=()):
    return jnp.concatenate([ref[lead + (pl.ds(c, n_rows, stride=SUBLANES), slice(None))]
                            for c in range(SUBLANES)], axis=1)


def _tile_rows_store(ref, x, lead=()):
    for c in range(SUBLANES):
        ref[lead + (pl.ds(c, x.shape[0], stride=SUBLANES), slice(None))] = x[:, c * LANE:(c + 1) * LANE]


def _params(sem, vmem=VMEM_LIMIT):
    return pltpu.CompilerParams(dimension_semantics=sem, vmem_limit_bytes=vmem)


def _two_part_specs(n_first, n_second, tm):
    return [pl.BlockSpec((tm, D_MODEL), lambda i: (jnp.minimum(i, n_first - 1), 0)),
            pl.BlockSpec((tm, D_MODEL), lambda i: (jnp.clip(i - n_first, 0, n_second - 1), 0))]


def _two_part_rows(n_first, a_ref, b_ref):
    return jnp.where(pl.program_id(0) < n_first, a_ref[...], b_ref[...])


def _inproj_kernel(n_first, xa_ref, xb_ref, g_ref, w_ref, wt_ref, qt_ref, vt_ref, *out_refs):
    u = _rms(_two_part_rows(n_first, xa_ref, xb_ref), g_ref[...]).astype(BF16)
    t = _dot_nt(wt_ref[...], u)
    vt_ref[...] = t[QT_ROWS:, :].astype(vt_ref.dtype)
    for h in range(N_HEADS_A):
        g, rep = divmod(h, REP_A)
        for bl in range(u.shape[0] // WINDOW):
            lane0 = (bl * REP_A + rep) * WINDOW
            qt_ref[g, :, lane0:lane0 + WINDOW] = t[h * HEAD_DIM_A:(h + 1) * HEAD_DIM_A,
                                                    bl * WINDOW:(bl + 1) * WINDOW].astype(qt_ref.dtype)
    off = 0
    for seg, (o_ref, width) in enumerate(zip(out_refs, SEG_WIDTHS)):
        for c0 in range(0, width, 1024):
            cw = min(1024, width - c0)
            y = _dot(u, w_ref[:, off + c0:off + c0 + cw])
            if seg == BQ_SEG:
                y = y * (DK_B ** -0.5)
            o_ref[:, c0:c0 + cw] = y.astype(o_ref.dtype)
        off += width


def _inproj(xa, xb, g, w, wt, tm=ROW_TILE_INPROJ):
    na, nb = xa.shape[0] // tm, xb.shape[0] // tm
    T = (na + nb) * tm
    ncols = w.shape[1]
    out_shape = [jax.ShapeDtypeStruct((N_KV_A, HEAD_DIM_A, REP_A * T), BF16),
                 jax.ShapeDtypeStruct((VT_ROWS, T), BF16)]
    out_shape += [jax.ShapeDtypeStruct((T, wd), dt) for wd, dt in zip(SEG_WIDTHS, SEG_DTYPES)]
    out_specs = [pl.BlockSpec((N_KV_A, HEAD_DIM_A, REP_A * tm), lambda i: (0, 0, i)),
                 pl.BlockSpec((VT_ROWS, tm), lambda i: (0, i))]
    out_specs += [pl.BlockSpec((tm, wd), lambda i: (i, 0)) for wd in SEG_WIDTHS]
    return pl.pallas_call(
        functools.partial(_inproj_kernel, na),
        grid=(na + nb,),
        in_specs=_two_part_specs(na, nb, tm) +
                 [pl.BlockSpec((1, D_MODEL), lambda i: (0, 0)),
                  pl.BlockSpec((D_MODEL, ncols), lambda i: (0, 0), pipeline_mode=pl.Buffered(1)),
                  pl.BlockSpec((QT_ROWS + VT_ROWS, D_MODEL), lambda i: (0, 0), pipeline_mode=pl.Buffered(1))],
        out_specs=out_specs,
        out_shape=out_shape,
        compiler_params=_params(("parallel",)),
        name="inproj",
    )(xa, xb, g, w, wt)


def _swa_kernel(qt_ref, kp_ref, kc_ref, kn_ref, vp_ref, vc_ref, vn_ref, bias_ref, band_ref, sink_ref,
                o_ref, ot_scr):
    i = pl.program_id(1)
    nb = pl.num_programs(1)
    tbl = (i == 0).astype(jnp.int32) + 2 * (i == nb - 1).astype(jnp.int32)
    kcat = jnp.concatenate([kp_ref[...], kc_ref[...], kn_ref[...]], axis=0)
    vcat = jnp.concatenate([vp_ref[...], vc_ref[...], vn_ref[...]], axis=1)
    for g in range(N_KV_A):
        gsl = slice(g * HEAD_DIM_A, (g + 1) * HEAD_DIM_A)
        s = _dot(kcat[:, gsl], qt_ref[g]) + bias_ref[g]
        s = jnp.where(band_ref[tbl] != 0.0, s, NEG_INF)
        sink = sink_ref[g]
        m = jnp.maximum(jnp.max(s, axis=0, keepdims=True), sink)
        p = jnp.exp(s - m)
        denom = jnp.sum(p, axis=0, keepdims=True) + jnp.exp(sink - m)
        ot_scr[gsl, :] = _dot(vcat[gsl, :], p.astype(BF16)) / denom
    for rep in range(REP_A):
        blk = ot_scr[:, rep * WINDOW:(rep + 1) * WINDOW]
        o_ref[:, rep * VT_ROWS:(rep + 1) * VT_ROWS] = blk.T.astype(o_ref.dtype)


def _swa(qt, k, vt, bias_t, band, sink_rows, B, S):
    nb = S // WINDOW
    kvw = N_KV_A * HEAD_DIM_A
    rw = REP_A * WINDOW

    def clampi(b, j):
        return b * nb + jnp.clip(j, 0, nb - 1)

    k_specs = [pl.BlockSpec((WINDOW, kvw), lambda b, i, d=d: (clampi(b, i + d), 0)) for d in (-1, 0, 1)]
    v_specs = [pl.BlockSpec((kvw, WINDOW), lambda b, i, d=d: (0, clampi(b, i + d))) for d in (-1, 0, 1)]
    const3 = lambda b, i: (0, 0, 0)
    return pl.pallas_call(
        _swa_kernel,
        grid=(B, nb),
        in_specs=[pl.BlockSpec((N_KV_A, HEAD_DIM_A, rw), lambda b, i: (0, 0, b * nb + i))] + k_specs + v_specs +
                 [pl.BlockSpec((N_KV_A, 3 * WINDOW, rw), const3),
                  pl.BlockSpec((4, 3 * WINDOW, rw), const3),
                  pl.BlockSpec((N_KV_A, 1, rw), const3)],
        out_specs=pl.BlockSpec((WINDOW, D_MODEL), lambda b, i: (b * nb + i, 0)),
        out_shape=jax.ShapeDtypeStruct((B * S, D_MODEL), BF16),
        scratch_shapes=[pltpu.VMEM((kvw, rw), F32)],
        compiler_params=_params(("parallel", "arbitrary")),
        name="swa",
    )(qt, k, k, k, vt, vt, vt, bias_t, band, sink_rows)


GLA_ROWS = 512
GLA_UNROLL = 8
GLA_PREP_UNROLL = 2


def _chunk_scan(g, pos, fwd):
    n = g.shape[0]
    s = 1
    while s < GLA_CHUNK:
        if fwd:
            g = g + jnp.where(pos >= s, pltpu.roll(g, s, axis=0), 0.0)
        else:
            g = g + jnp.where(pos < GLA_CHUNK - s, pltpu.roll(g, n - s, axis=0), 0.0)
        s *= 2
    return g


def _gla_kernel(q_ref, k_ref, v_ref, br_ref, blr_ref, gu_ref, gb_ref, gn_ref, o_ref,
                qd_scr, a_scr, kv_scr, dl_scr, acc_scr, st_scr):
    S = q_ref.shape[0]
    C = GLA_CHUNK
    nc = S // C
    R = GLA_ROWS
    cpr = R // C
    ri = lax.broadcasted_iota(jnp.int32, (C, C), 0)
    ci = lax.broadcasted_iota(jnp.int32, (C, C), 1)
    pos = lax.broadcasted_iota(jnp.int32, (R, DK_B), 0) & (C - 1)

    for d in range(2):
        fwd = d == 0

        def prep(t, carry):
            sl = pl.ds(pl.multiple_of(t * R, R), R)
            z = _dot(blr_ref[sl, :].astype(BF16), gu_ref[d]) + gb_ref[d]
            g = (jnp.minimum(z, 0.0) - jnp.log(1.0 + jnp.exp(-jnp.abs(z)))) * (1.0 / GATE_TEMP)
            b = _chunk_scan(g, pos, fwd).reshape(cpr, C, DK_B)
            b_end = b[:, C - 1:C, :] if fwd else b[:, 0:1, :]
            q = q_ref[sl, :].astype(F32).reshape(cpr, C, DK_B)
            k = k_ref[sl, :].astype(F32).reshape(cpr, C, DK_B)
            qd = (q * jnp.exp(b)).astype(BF16)
            ki = (k * jnp.exp(-b)).astype(BF16)
            ks = (k * jnp.exp(b_end - b)).astype(BF16)
            qd_scr[sl, :] = qd.reshape(R, DK_B)
            dl_scr[pl.ds(pl.multiple_of(t * cpr, cpr), cpr)] = jnp.broadcast_to(jnp.exp(b_end), (cpr, 8, DK_B))
            for c in range(cpr):
                csl = pl.ds(pl.multiple_of(t * R + c * C, C), C)
                a = _dot_nt(qd[c], ki[c])
                a_scr[csl, :] = jnp.where((ri >= ci) if fwd else (ri < ci), a, 0.0).astype(BF16)
                kv_scr[t * cpr + c] = _dot_tn(v_ref[csl, :], ks[c])
            return carry

        lax.fori_loop(0, S // R, prep, 0, unroll=GLA_PREP_UNROLL)
        st_scr[...] = jnp.zeros_like(st_scr)

        def chunk(t, carry):
            n = t if fwd else nc - 1 - t
            sl = pl.ds(pl.multiple_of(n * C, C), C)
            st = st_scr[...]
            o = _dot(a_scr[sl, :], v_ref[sl, :]) + _dot_nt(qd_scr[sl, :], st.astype(BF16))
            st_scr[...] = st * dl_scr[n][0:1, :] + kv_scr[n]
            if fwd:
                acc_scr[sl, :] = o
            else:
                o = o + acc_scr[sl, :]
                o = o * lax.rsqrt(jnp.mean(o * o, axis=-1, keepdims=True) + EPS) * gn_ref[...]
                r = br_ref[sl, :].astype(F32)
                o_ref[sl, :] = (o * (r * _sigmoid(r))).astype(o_ref.dtype)
            return carry

        lax.fori_loop(0, nc, chunk, 0, unroll=GLA_UNROLL)


def _gla(bq, bk, bv, br, blr, gu, gb, gn):
    B, S, _ = bq.shape
    nc = S // GLA_CHUNK
    return pl.pallas_call(
        _gla_kernel,
        grid=(B, N_HEADS_B),
        in_specs=[pl.BlockSpec((None, S, DK_B), lambda b, h: (b, 0, h)),
                  pl.BlockSpec((None, S, DK_B), lambda b, h: (b, 0, h)),
                  pl.BlockSpec((None, S, DV_B), lambda b, h: (b, 0, h)),
                  pl.BlockSpec((None, S, DV_B), lambda b, h: (b, 0, h)),
                  pl.BlockSpec((None, S, LANE), lambda b, h: (b, 0, 0)),
                  pl.BlockSpec((2, None, LANE, DK_B), lambda b, h: (0, h, 0, 0)),
                  pl.BlockSpec((2, None, 1, DK_B), lambda b, h: (0, h, 0, 0)),
                  pl.BlockSpec((None, 1, DV_B), lambda b, h: (h, 0, 0))],
        out_specs=pl.BlockSpec((None, S, DV_B), lambda b, h: (b, 0, h)),
        out_shape=jax.ShapeDtypeStruct((B, S, N_HEADS_B * DV_B), BF16),
        scratch_shapes=[pltpu.VMEM((S, DK_B), BF16), pltpu.VMEM((S, GLA_CHUNK), BF16),
                        pltpu.VMEM((nc, DV_B, DK_B), F32),
                        pltpu.VMEM((nc, 8, DK_B), F32),
                        pltpu.VMEM((S, DV_B), F32),
                        pltpu.VMEM((DV_B, DK_B), F32)],
        compiler_params=_params(("parallel", "arbitrary")),
        name="gla",
    )(bq, bk, bv, br, blr, gu, gb, gn)


def _postmix_kernel(n_first, ao_ref, ob_ref, gates_ref, xa_ref, xb_ref, wa_ref, wb_ref, wo_ref, h_ref):
    ya = _dot(ao_ref[...], wa_ref[...])
    yb = _dot(ob_ref[...], wb_ref[...])
    ga = gates_ref[:, :D_MODEL].astype(F32)
    gb = gates_ref[:, D_MODEL:].astype(F32)
    merged = _sigmoid(ga) * ya + _sigmoid(gb) * yb
    h_ref[...] = _two_part_rows(n_first, xa_ref, xb_ref) + _dot(merged.astype(BF16), wo_ref[...])


def _postmix(ao, ob, gates, xa, xb, wa, wb, wo, tm=ROW_TILE_POSTMIX):
    na, nb = xa.shape[0] // tm, xb.shape[0] // tm
    T = (na + nb) * tm
    row = lambda i: (i, 0)
    const = lambda i: (0, 0)
    wspec = pl.BlockSpec((D_MODEL, D_MODEL), const)
    return pl.pallas_call(
        functools.partial(_postmix_kernel, na),
        grid=(na + nb,),
        in_specs=[pl.BlockSpec((tm, D_MODEL), row), pl.BlockSpec((tm, D_MODEL), row),
                  pl.BlockSpec((tm, 2 * D_MODEL), row)] + _two_part_specs(na, nb, tm) +
                 [wspec, wspec, wspec],
        out_specs=pl.BlockSpec((tm, D_MODEL), row),
        out_shape=jax.ShapeDtypeStruct((T, D_MODEL), F32),
        compiler_params=_params(("parallel",)),
        name="postmix",
    )(ao, ob, gates, xa, xb, wa, wb, wo)


def _memkv_kernel(m_ref, g_ref, w_ref, k_ref, v_ref):
    u = _rms(m_ref[...], g_ref[...]).astype(BF16)
    k_ref[...] = _dot(u, w_ref[:, :D_MODEL]).astype(k_ref.dtype)
    v_ref[...] = _dot(u, w_ref[:, D_MODEL:]).astype(v_ref.dtype)


def _memkv(mem, g, w):
    B, M, _ = mem.shape
    blk = pl.BlockSpec((None, M, D_MODEL), lambda b: (b, 0, 0))
    return pl.pallas_call(
        _memkv_kernel,
        grid=(B,),
        in_specs=[blk, pl.BlockSpec((1, D_MODEL), lambda b: (0, 0)),
                  pl.BlockSpec((D_MODEL, 2 * D_MODEL), lambda b: (0, 0))],
        out_specs=[blk, blk],
        out_shape=[jax.ShapeDtypeStruct((B, M, D_MODEL), BF16)] * 2,
        compiler_params=_params(("parallel",)),
        name="memkv",
    )(mem, g, w)


def _xattn_kernel(h_ref, k_ref, v_ref, gx_ref, wq_ref, wo_ref, gm_ref, wr_ref, br_ref,
                  h2_ref, u3_ref, rt_ref):
    h1 = h_ref[...]
    u2 = _rms(h1, gx_ref[...]).astype(BF16)
    qx = _dot(u2, wq_ref[...]).astype(BF16)
    scale = HEAD_DIM_X ** -0.5
    outs = []
    for hh in range(N_HEADS_X):
        sl = slice(hh * HEAD_DIM_X, (hh + 1) * HEAD_DIM_X)
        s = _dot_nt(qx[:, sl], k_ref[:, sl]) * scale
        m = jnp.max(s, axis=-1, keepdims=True)
        p = jnp.exp(s - m)
        p = p / jnp.sum(p, axis=-1, keepdims=True)
        outs.append(_dot(p.astype(BF16), v_ref[:, sl]).astype(BF16))
    o = jnp.concatenate(outs, axis=-1)
    h2 = h1 + _dot(o, wo_ref[...])
    h2_ref[...] = h2
    u3 = _rms(h2, gm_ref[...])
    _tile_rows_store(u3_ref, u3)
    u_hi = u3.astype(BF16)
    u_lo = (u3 - u_hi.astype(F32)).astype(BF16)
    hh_hl = _dot(u_hi, wr_ref[...])
    lg = hh_hl[:, :ROUTER_LANES] + (hh_hl[:, ROUTER_LANES:] + _dot(u_lo, wr_ref[:, :ROUTER_LANES]))
    rt_ref[...] = _route_lanes(lg + br_ref[...])


def _route_lanes(lg):
    lane = lax.broadcasted_iota(jnp.int32, lg.shape, 1)
    first = lambda hit: jnp.min(jnp.where(hit, lane, ROUTER_LANES), axis=-1, keepdims=True)
    gl = jnp.where(lane < N_GROUPS, lg, -jnp.inf)
    gmax = jnp.max(gl, axis=-1, keepdims=True)
    grp = first(gl == gmax)
    p_grp = 1.0 / jnp.sum(jnp.exp(gl - gmax), axis=-1, keepdims=True)
    lo = N_GROUPS + grp * EXPERTS_PER_GROUP
    el = jnp.where((lane >= lo) & (lane < lo + EXPERTS_PER_GROUP), lg, -jnp.inf)
    m1 = jnp.max(el, axis=-1, keepdims=True)
    i1 = first(el == m1)
    el = jnp.where(lane == i1, -jnp.inf, el)
    m2 = jnp.max(el, axis=-1, keepdims=True)
    i2 = first(el == m2)
    e2 = jnp.exp(m2 - m1)
    w1 = p_grp / (1.0 + e2)
    w2 = p_grp * e2 / (1.0 + e2)
    out = jnp.where(lane == 0, w1, jnp.where(lane == 1, w2, 0.0))
    out = jnp.where(lane == 2, (i1 - N_GROUPS).astype(F32), out)
    return jnp.where(lane == 3, (i2 - N_GROUPS).astype(F32), out)


def _xattn(h1, mk, mv, gx, wq, wo, gm, wr, brt, tm=ROW_TILE_XATTN):
    B, S, _ = h1.shape
    M = mk.shape[1]
    row = lambda b, i: (b, i, 0)
    flat = lambda b, i: (b * (S // tm) + i, 0)
    c2 = lambda b, i: (0, 0)
    kv = pl.BlockSpec((None, M, D_MODEL), lambda b, i: (b, 0, 0))
    vec = pl.BlockSpec((1, D_MODEL), c2)
    wsq = pl.BlockSpec((D_MODEL, D_MODEL), c2)
    wr_spec = pl.BlockSpec((D_MODEL, 2 * ROUTER_LANES), c2)
    return pl.pallas_call(
        _xattn_kernel,
        grid=(B, S // tm),
        in_specs=[pl.BlockSpec((None, tm, D_MODEL), row), kv, kv, vec, wsq, wsq, vec, wr_spec,
                  pl.BlockSpec((1, ROUTER_LANES), c2)],
        out_specs=[pl.BlockSpec((tm, D_MODEL), flat), pl.BlockSpec((tm * SUBLANES, LANE), flat),
                   pl.BlockSpec((tm, ROUTER_LANES), flat)],
        out_shape=[jax.ShapeDtypeStruct((B * S, D_MODEL), F32), jax.ShapeDtypeStruct((B * S * SUBLANES, LANE), F32),
                   jax.ShapeDtypeStruct((B * S, ROUTER_LANES), F32)],
        compiler_params=_params(("parallel", "arbitrary")),
        name="xattn",
    )(h1, mk, mv, gx, wq, wo, gm, wr, brt)


def _moe_kernel(be_ref, nrows_ref, src_ref, srcn_ref, dst_ref, u_hbm, wg_ref, wu_ref, wd_ref, o_hbm,
                xbuf, obuf, sem_in, sem_out):
    j = pl.program_id(0)
    nb = pl.num_programs(0)
    slot = j & 1
    n_cur = nrows_ref[j]

    def tile(r):
        return pl.ds(r * SUBLANES, SUBLANES)

    def gather_row(idx_ref, s, r, wait):
        t = 0 if wait else pl.multiple_of(idx_ref[0, 0, r], SUBLANES)
        cp = pltpu.make_async_copy(u_hbm.at[pl.ds(t, SUBLANES), :], xbuf.at[s, tile(r), :], sem_in.at[s])
        cp.wait() if wait else cp.start(priority=dma_thread(r))

    def scatter_row(s, r, wait):
        t = 0 if wait else pl.multiple_of(dst_ref[0, 0, r], SUBLANES)
        cp = pltpu.make_async_copy(obuf.at[s, tile(r), :], o_hbm.at[pl.ds(t, SUBLANES), :], sem_out.at[s])
        cp.wait() if wait else cp.start(priority=dma_thread(r))

    def dma_thread(r):
        return r % 2 if isinstance(r, int) else 0

    def scatter(s, n, wait):
        @pl.when(n == MOE_ROWS)
        def _():
            for r in range(MOE_ROWS):
                scatter_row(s, r, wait)

        @pl.when((n > 0) & (n < MOE_ROWS))
        def _():
            def body(r, carry):
                scatter_row(s, r, wait)
                return carry
            lax.fori_loop(0, n, body, 0)

    @pl.when((j == 0) & (n_cur > 0))
    def _():
        for r in range(MOE_ROWS):
            gather_row(src_ref, 0, r, False)

    @pl.when(j >= 2)
    def _():
        scatter(slot, nrows_ref[jnp.maximum(j - 2, 0)], True)

    @pl.when((j + 1 < nb) & (nrows_ref[jnp.minimum(j + 1, nb - 1)] > 0))
    def _():
        for r in range(MOE_ROWS):
            gather_row(srcn_ref, 1 - slot, r, False)

    @pl.when(n_cur > 0)
    def _():
        for r in range(MOE_ROWS):
            gather_row(None, slot, r, True)
        x = _tile_rows_load(xbuf, MOE_ROWS, (slot,)).astype(BF16)
        g = _dot(x, wg_ref[...])
        u = _dot(x, wu_ref[...])
        hid = (g * _sigmoid(g)) * u
        _tile_rows_store(obuf, _dot(hid.astype(BF16), wd_ref[...]), (slot,))
    scatter(slot, n_cur, False)

    @pl.when(j == nb - 1)
    def _():
        @pl.when(j >= 1)
        def _():
            scatter(1 - slot, nrows_ref[jnp.maximum(j - 1, 0)], True)
        scatter(slot, n_cur, True)


def _moe(block_e, n_rows, src, dst, u3, wg, wu, wd, n_out_rows):
    n_blocks = src.shape[0]
    idx = pl.BlockSpec((1, 1, MOE_ROWS), lambda j, be, nu: (j, 0, 0), memory_space=pltpu.SMEM)
    idx_next = pl.BlockSpec((1, 1, MOE_ROWS), lambda j, be, nu: (jnp.minimum(j + 1, n_blocks - 1), 0, 0),
                            memory_space=pltpu.SMEM)
    grid_spec = pltpu.PrefetchScalarGridSpec(
        num_scalar_prefetch=2,
        grid=(n_blocks,),
        in_specs=[idx, idx_next, idx,
                  pl.BlockSpec(memory_space=pl.ANY),
                  pl.BlockSpec((None, D_MODEL, D_EXPERT), lambda j, be, nu: (be[j], 0, 0)),
                  pl.BlockSpec((None, D_MODEL, D_EXPERT), lambda j, be, nu: (be[j], 0, 0)),
                  pl.BlockSpec((None, D_EXPERT, D_MODEL), lambda j, be, nu: (be[j], 0, 0))],
        out_specs=pl.BlockSpec(memory_space=pl.ANY),
        scratch_shapes=[pltpu.VMEM((2, MOE_ROWS * SUBLANES, LANE), F32),
                        pltpu.VMEM((2, MOE_ROWS * SUBLANES, LANE), F32),
                        pltpu.SemaphoreType.DMA((2,)), pltpu.SemaphoreType.DMA((2,))])
    return pl.pallas_call(
        _moe_kernel,
        grid_spec=grid_spec,
        out_shape=jax.ShapeDtypeStruct((n_out_rows * SUBLANES, LANE), F32),
        compiler_params=_params(("arbitrary",)),
        name="moe",
    )(block_e, n_rows, src, src, dst, u3, wg, wu, wd)


def _combine_kernel(o0_ref, o1_ref, rt_ref, h_ref, g_ref, y_ref):
    n = h_ref.shape[0]
    y = _tile_rows_load(o0_ref, n) * rt_ref[:, 0:1] + _tile_rows_load(o1_ref, n) * rt_ref[:, 1:2]
    y_ref[...] = _rms(h_ref[...] + y, g_ref[...])


def _combine(o2, rt, h2, g, row0, n_rows, tm=ROW_TILE_COMBINE):
    T = h2.shape[0]
    first, k1 = row0 // tm, T // tm
    row = lambda i: (i + first, 0)
    return pl.pallas_call(
        _combine_kernel,
        grid=(n_rows // tm,),
        in_specs=[pl.BlockSpec((tm * SUBLANES, LANE), row),
                  pl.BlockSpec((tm * SUBLANES, LANE), lambda i: (i + first + k1, 0)),
                  pl.BlockSpec((tm, ROUTER_LANES), row), pl.BlockSpec((tm, D_MODEL), row),
                  pl.BlockSpec((1, D_MODEL), lambda i: (0, 0))],
        out_specs=pl.BlockSpec((tm, D_MODEL), lambda i: (i, 0)),
        out_shape=jax.ShapeDtypeStruct((n_rows, D_MODEL), F32),
        compiler_params=_params(("parallel",)),
        name="combine",
    )(o2, o2, rt, h2, g)


def _t5_bucket(rel):
    half = N_BUCKETS // 2
    max_exact = half // 2
    n = jnp.abs(rel)
    large = max_exact + (jnp.log(jnp.maximum(n, 1).astype(jnp.float32) / max_exact)
                         / math.log(MAX_DISTANCE / max_exact) * (half - max_exact)).astype(jnp.int32)
    large = jnp.minimum(large, half - 1)
    return jnp.where(rel > 0, half, 0) + jnp.where(n < max_exact, n, large)


def _window_tables(rel_bias, attn_sink):
    c = jnp.arange(3 * WINDOW)[:, None]
    r = jnp.arange(WINDOW)[None, :]
    bucket = _t5_bucket(c - WINDOW - r)
    rb = rel_bias.astype(F32)
    bias = jnp.zeros((N_HEADS_A, 3 * WINDOW, WINDOW), F32)
    for b in range(N_BUCKETS):
        bias = jnp.where((bucket == b)[None], rb[b][:, None, None], bias)
    bias_t = bias.reshape(N_KV_A, REP_A, 3 * WINDOW, WINDOW).transpose(0, 2, 1, 3)
    bias_t = bias_t.reshape(N_KV_A, 3 * WINDOW, REP_A * WINDOW)
    in_band = jnp.tile(jnp.abs(c - WINDOW - r) <= WINDOW, (1, REP_A))
    has_prev = c >= WINDOW
    has_next = c < 2 * WINDOW
    band = jnp.stack([in_band, in_band & has_prev, in_band & has_next, in_band & has_prev & has_next])
    sink_rows = jnp.repeat(attn_sink.astype(F32).reshape(N_KV_A, REP_A), WINDOW, axis=1)
    return bias_t, band.astype(F32), sink_rows.reshape(N_KV_A, 1, REP_A * WINDOW)


def _split_cols(w, sizes):
    out, acc = [], 0
    for s in sizes:
        out.append(w[:, acc:acc + s])
        acc += s
    return out


def _dispatch_tables(eid, T):
    A = T * TOP_K
    assert A % MOE_ROWS == 0
    n_pad = N_EXPERTS * MOE_ROWS
    n_blocks = A // MOE_ROWS + N_EXPERTS
    id_bits = max(1, (A - 1).bit_length())
    assert id_bits + 7 < 31
    e_flat = eid.reshape(A)
    experts = jnp.arange(N_EXPERTS, dtype=jnp.int32)
    counts = jnp.sum((e_flat[:, None] == experts[None, :]).astype(jnp.int32), axis=0)
    padded = (counts + MOE_ROWS - 1) // MOE_ROWS * MOE_ROWS
    pend = jnp.cumsum(padded)
    pstart = pend - padded
    blk_first = jnp.arange(n_blocks, dtype=jnp.int32) * MOE_ROWS
    block_e = jnp.minimum(jnp.sum((pend[None, :] <= blk_first[:, None]).astype(jnp.int32), axis=1), N_EXPERTS - 1)
    n_rows = jnp.clip(counts[block_e] - (blk_first - pstart[block_e]), 0, MOE_ROWS).astype(jnp.int32)
    pad_e = jnp.repeat(experts, MOE_ROWS)
    pad_i = jnp.tile(jnp.arange(MOE_ROWS, dtype=jnp.int32), N_EXPERTS)
    pad_key = jnp.where(pad_i < (padded - counts)[pad_e], 2 * pad_e + 1, 2 * N_EXPERTS)
    words = jnp.concatenate([(2 * e_flat << id_bits) | jnp.arange(A, dtype=jnp.int32), pad_key << id_bits])
    asg = jnp.sort(words) & ((1 << id_bits) - 1)
    src = asg // TOP_K
    dst = (asg % TOP_K) * T + src
    return (block_e.astype(jnp.int32), n_rows,
            (src * SUBLANES).reshape(n_blocks, 1, MOE_ROWS), (dst * SUBLANES).reshape(n_blocks, 1, MOE_ROWS))


def kernel(x_prompt, x_sample, mem_prompt, mem_sample, rel_bias, norm_mix, w_in, attn_sink, gla_gate_up, gla_gate_bias, gla_norm, w_branch_a, w_branch_b, w_out, norm_xattn, norm_mem, w_xq, w_xkv, w_xo, norm_moe, w_router_group, b_router_group, w_router_expert, b_router_expert, w_expert_gate, w_expert_up, w_expert_down, norm_final):
    assert norm_mix.shape[0] == 1, "single-layer trunk"
    Bp, S, D = x_prompt.shape
    Bs = x_sample.shape[0]
    assert x_sample.shape[1] == S and D == D_MODEL
    B = Bp + Bs
    T = B * S
    l = 0

    xp = x_prompt.reshape(Bp * S, D)
    xs = x_sample.reshape(Bs * S, D)
    mem = jnp.concatenate([mem_prompt, mem_sample], axis=0)

    in_sizes = (N_HEADS_A * HEAD_DIM_A, N_KV_A * HEAD_DIM_A, N_KV_A * HEAD_DIM_A,
                N_HEADS_B * DK_B, N_HEADS_B * DK_B, N_HEADS_B * DV_B, N_HEADS_B * DV_B,
                2 * GATE_RANK, 2 * D_MODEL)
    w_aq, w_ak, w_av, w_bq, w_bk, w_bv, w_br, w_blr, w_g = _split_cols(w_in[l], in_sizes)
    w_blr = jnp.pad(w_blr, ((0, 0), (0, LANE - 2 * GATE_RANK)))
    w1 = jnp.concatenate([w_ak, w_bq, w_bk, w_bv, w_br, w_g, w_blr], axis=1).astype(BF16)
    assert math.log2(HEAD_DIM_A) % 2 == 0
    wt = jnp.concatenate([w_aq.T * (HEAD_DIM_A ** -0.5), w_av.T], axis=0).astype(BF16)
    wa = w_branch_a[l].reshape(N_KV_A, REP_A, HEAD_DIM_A, D).transpose(1, 0, 2, 3).reshape(D, D).astype(BF16)
    gu = gla_gate_up[l].reshape(2, GATE_RANK, N_HEADS_B, DK_B).transpose(0, 2, 1, 3)
    gu_pad = jnp.zeros((2, N_HEADS_B, LANE, DK_B), F32)
    gu_pad = gu_pad.at[0, :, 0:GATE_RANK].set(gu[0]).at[1, :, GATE_RANK:2 * GATE_RANK].set(gu[1]).astype(BF16)
    gb = gla_gate_bias[l].reshape(2, N_HEADS_B, 1, DK_B).astype(F32)
    gn = gla_norm[l].reshape(N_HEADS_B, 1, DV_B).astype(F32)
    bias_t, band, sink_rows = _window_tables(rel_bias, attn_sink[l])
    w_r = jnp.concatenate([w_router_group[l], w_router_expert[l]], axis=1).astype(F32)
    w_r = jnp.pad(w_r, ((0, 0), (0, ROUTER_LANES - w_r.shape[1])))
    w_r_hi = w_r.astype(BF16)
    w_r_lo = (w_r - w_r_hi.astype(F32)).astype(BF16)
    b_r = jnp.concatenate([b_router_group[l], b_router_expert[l]]).astype(F32)
    b_r = jnp.pad(b_r, (0, ROUTER_LANES - b_r.shape[0])).reshape(1, ROUTER_LANES)

    qt, vt, ak, bq, bk, bv, br, gates, blr = _inproj(xp, xs, norm_mix[l].reshape(1, D), w1, wt)
    r3 = lambda t: t.reshape(B, S, t.shape[-1])
    ao = _swa(qt, ak, vt, bias_t, band, sink_rows, B, S)
    ob = _gla(r3(bq), r3(bk), r3(bv), r3(br), r3(blr), gu_pad, gb, gn)
    h1 = _postmix(ao, ob.reshape(T, D), gates, xp, xs,
                  wa, w_branch_b[l].astype(BF16), w_out[l].astype(BF16))

    mk, mv = _memkv(mem, norm_mem[l].reshape(1, D), w_xkv[l].astype(BF16))
    h2, u3, rt = _xattn(h1.reshape(B, S, D), mk, mv, norm_xattn[l].reshape(1, D),
                        w_xq[l].astype(BF16), w_xo[l].astype(BF16), norm_moe[l].reshape(1, D),
                        jnp.concatenate([w_r_hi, w_r_lo], axis=1), b_r)

    block_e, n_rows, src, dst = _dispatch_tables(rt[:, TOP_K:2 * TOP_K].astype(jnp.int32), T)
    o2 = _moe(block_e, n_rows, src, dst, u3,
              w_expert_gate[l].astype(BF16), w_expert_up[l].astype(BF16), w_expert_down[l].astype(BF16),
              T * TOP_K)
    g_final = norm_final.reshape(1, D)
    y_prompt = _combine(o2, rt, h2, g_final, 0, Bp * S)
    y_sample = _combine(o2, rt, h2, g_final, Bp * S, Bs * S)
    return (y_prompt.reshape(Bp, S, D), y_sample.reshape(Bs, S, D))
```
